```python
import math
import jax, jax.numpy as jnp
from jax import lax
import numpy as np

D_MODEL = 1024
BATCH = 2
SEQ = 8192
DEPTH = 2
DEC_BATCH = 32
DEC_SEQ = 4
PAST_LEN = 8192
PAGE_SIZE = 128

N_A_LAYERS = DEPTH // 2
N_B_LAYERS = DEPTH - N_A_LAYERS
RWKV_HEAD_DIM = 64
RWKV_HEADS = D_MODEL // RWKV_HEAD_DIM
DECAY_LORA = 64
ICLR_LORA = 64
GATE_LORA = 128
N_HEADS = 16
HEAD_DIM = D_MODEL // N_HEADS
N_KV_HEADS = N_HEADS
MOBA_BLOCK = 256
MOBA_TOPK = 3
Q_CHUNK = 64
D_FF = -(-8 * D_MODEL // (3 * 256)) * 256
PLE_DIM = 256
RMS_EPS = 1e-6
LNX_EPS = 64e-5

kernel_name = 'rwkv7_moba_yoco_decode_step'


def rmsnorm(x, g):
    xf = x.astype(jnp.float32)
    y = xf * lax.rsqrt(jnp.mean(xf * xf, axis=-1, keepdims=True) + RMS_EPS)
    return (y * g.astype(jnp.float32)).astype(x.dtype)


def swiglu(x, w_gate, w_up, w_down):
    return (jax.nn.silu(x @ w_gate) * (x @ w_up)) @ w_down


def rwkv7_time_mix(xn, shift_prev, wkv0, mu, w_rkv, w_o, w0, w1, w2, a0, a1, a2,
                   g1, g2, k_k, k_a, r_k, lnx_w, lnx_b):
    f32 = jnp.float32
    b, t, d = xn.shape
    x_prev = jnp.concatenate([shift_prev[:, None, :].astype(xn.dtype), xn[:, :-1]], axis=1)
    xx = x_prev - xn
    xr = xn + xx * mu[0]
    xw = xn + xx * mu[1]
    xk = xn + xx * mu[2]
    xv = xn + xx * mu[3]
    xa = xn + xx * mu[4]
    xg = xn + xx * mu[5]
    r = xr @ w_rkv[0]
    k = xk @ w_rkv[1]
    v = xv @ w_rkv[2]
    w_log = -jax.nn.softplus(-(w0 + jnp.tanh(xw @ w1) @ w2).astype(f32)) - 0.5
    decay = jnp.exp(-jnp.exp(w_log))
    a = jax.nn.sigmoid((a0 + (xa @ a1) @ a2).astype(f32))
    g = jax.nn.sigmoid(xg @ g1) @ g2

    def heads(z):
        return z.astype(f32).reshape(b, t, RWKV_HEADS, RWKV_HEAD_DIM)

    kk = heads(k * k_k)
    kk = kk / jnp.maximum(jnp.linalg.norm(kk, axis=-1, keepdims=True), 1e-12)
    k_mod = k.astype(f32) * (1.0 + (a - 1.0) * k_a.astype(f32))
    r_h, w_h, k_h, v_h, a_h = heads(r), heads(decay), heads(k_mod), heads(v), heads(a)

    def step(s, inp):
        r_t, w_t, k_t, v_t, kk_t, a_t = inp
        sa = jnp.einsum('bhvk,bhk->bhv', s, -kk_t)
        s = (s * w_t[:, :, None, :] + sa[..., None] * (kk_t * a_t)[:, :, None, :]
             + v_t[..., None] * k_t[:, :, None, :])
        return s, jnp.einsum('bhvk,bhk->bhv', s, r_t)

    seq = tuple(jnp.swapaxes(z, 0, 1) for z in (r_h, w_h, k_h, v_h, kk, a_h))
    s_fin, o = lax.scan(step, wkv0.astype(f32), seq)
    o = jnp.swapaxes(o, 0, 1)
    mean = jnp.mean(o, axis=-1, keepdims=True)
    var = jnp.mean(jnp.square(o - mean), axis=-1, keepdims=True)
    o = ((o - mean) * lax.rsqrt(var + LNX_EPS)).reshape(b, t, d) * lnx_w + lnx_b
    bonus = jnp.sum(r_h * k_h * r_k.astype(f32), axis=-1, keepdims=True) * v_h
    o = o + bonus.reshape(b, t, d)
    out = (o * g.astype(f32)).astype(xn.dtype) @ w_o
    return out, s_fin, xn[:, -1]


def alibi_slopes():
    return jnp.exp2(-8.0 * jnp.arange(1, N_HEADS + 1, dtype=jnp.float32) / N_HEADS)


def kv_blocks(parts):
    parts = [z.astype(jnp.float32) for z in parts]
    b, _, hk, dh = parts[0].shape
    length = sum(z.shape[1] for z in parts)
    nb = -(-length // MOBA_BLOCK)
    pad = nb * MOBA_BLOCK - length
    if pad:
        parts = parts + [jnp.zeros((b, pad, hk, dh), jnp.float32)]
    z = jnp.concatenate(parts, axis=1) if len(parts) > 1 else parts[0]
    return z.reshape(b, nb, MOBA_BLOCK, hk, dh)


def moba_core(q, qpos, kb, vb, kmean, slopes):
    b, h, tq, dh = q.shape
    nb = kb.shape[1]
    ksel = min(MOBA_TOPK, nb)
    cur = qpos // MOBA_BLOCK
    gate = jnp.einsum('bhqd,bnhd->bhqn', q, kmean)
    past = jnp.arange(nb)[None, :] < cur[:, None]
    gate = jnp.where(past[None, None], gate, -jnp.inf)
    _, sel = lax.top_k(gate, ksel)
    own = jnp.broadcast_to(cur[None, None, :, None], (b, h, tq, 1)).astype(sel.dtype)
    idx = jnp.concatenate([sel, own], axis=-1)
    bi = jnp.arange(b)[:, None, None, None]
    hi = jnp.arange(h)[None, :, None, None]
    kg = kb[bi, idx, :, hi]
    vg = vb[bi, idx, :, hi]
    s = jnp.einsum('bhqd,bhqnkd->bhqnk', q, kg) * (1.0 / math.sqrt(dh))
    kpos = idx[..., None] * MOBA_BLOCK + jnp.arange(MOBA_BLOCK, dtype=idx.dtype)
    dist = qpos[None, None, :, None, None] - kpos
    slot = jnp.arange(ksel + 1)
    sel_ok = (slot[None, :] < cur[:, None]) & (slot[None, :] < ksel)
    is_own = (slot == ksel)[None, None, None, :, None]
    valid = jnp.where(is_own, dist >= 0, sel_ok[None, None, :, :, None])
    s = s - slopes[None, :, None, None, None] * dist.astype(jnp.float32)
    s = jnp.where(valid, s, -jnp.inf)
    p = jax.nn.softmax(s.reshape(b, h, tq, -1), axis=-1).reshape(s.shape)
    return jnp.einsum('bhqnk,bhqnkd->bhqd', p, vg)


def moba_attend(q, pos0, kb, vb, kmean, slopes):
    b, t, h, dh = q.shape
    qh = q.astype(jnp.float32).transpose(0, 2, 1, 3)
    qpos = pos0 + jnp.arange(t, dtype=jnp.int32)
    if t <= Q_CHUNK:
        o = moba_core(qh, qpos, kb, vb, kmean, slopes)
    else:
        nc = t // Q_CHUNK
        qc = qh.reshape(b, h, nc, Q_CHUNK, dh).transpose(2, 0, 1, 3, 4)
        pc = qpos.reshape(nc, Q_CHUNK)
        o = lax.map(lambda a: moba_core(a[0], a[1], kb, vb, kmean, slopes), (qc, pc))
        o = o.transpose(1, 2, 0, 3, 4).reshape(b, h, t, dh)
    return o.transpose(0, 2, 1, 3).reshape(b, t, h * dh)


def trunk(x, p, wkv_in, shift_in, past_k, past_v, pos0, W):
    b, t, _ = x.shape
    slopes = alibi_slopes()
    h = x
    wkv_out, shift_out = [], []
    kb = vb = kmean = k_new = v_new = None
    for i in range(DEPTH):
        hn = rmsnorm(h, W['norm_mix'][i])
        if i < N_A_LAYERS:
            out, s_fin, sh = rwkv7_time_mix(
                hn, shift_in[i], wkv_in[i], W['rwkv_mu'][i], W['rwkv_w_rkv'][i], W['rwkv_w_o'][i],
                W['rwkv_w0'][i], W['rwkv_w1'][i], W['rwkv_w2'][i], W['rwkv_a0'][i], W['rwkv_a1'][i],
                W['rwkv_a2'][i], W['rwkv_g1'][i], W['rwkv_g2'][i], W['rwkv_k_k'][i], W['rwkv_k_a'][i],
                W['rwkv_r_k'][i], W['rwkv_lnx_w'][i], W['rwkv_lnx_b'][i])
            wkv_out.append(s_fin)
            shift_out.append(sh)
        else:
            j = i - N_A_LAYERS
            q = (hn @ W['attn_w_q'][j]).reshape(b, t, N_HEADS, HEAD_DIM)
            out = moba_attend(q, pos0, kb, vb, kmean, slopes).astype(h.dtype) @ W['attn_w_o'][j]
        h = h + out
        h = h + swiglu(rmsnorm(h, W['norm_ffn'][i]), W['ffn_w_gate'][i], W['ffn_w_up'][i], W['ffn_w_down'][i])
        gate = jax.nn.sigmoid(rmsnorm(h, W['norm_ple'][i]) @ W['ple_w_gate'][i])
        h = h + (p[i].astype(h.dtype) @ W['ple_w_proj'][i]) * gate
        if i == N_A_LAYERS - 1:
            hk = rmsnorm(h, W['norm_kv'])
            k_new = (hk @ W['kv_w_k']).reshape(b, t, N_KV_HEADS, HEAD_DIM)
            v_new = (hk @ W['kv_w_v']).reshape(b, t, N_KV_HEADS, HEAD_DIM)
            k_parts = [k_new] if past_k is None else [past_k, k_new]
            v_parts = [v_new] if past_v is None else [past_v, v_new]
            kb = kv_blocks(k_parts)
            vb = kv_blocks(v_parts)
            kmean = jnp.mean(kb, axis=2)
    y = rmsnorm(h, W['norm_final'])
    return y, jnp.stack(wkv_out), jnp.stack(shift_out), k_new, v_new


def setup_inputs(seed: int = 0) -> dict:
    key = jax.random.key(seed)
    ks = iter(jax.random.split(key, 48))
    f32 = jnp.float32

    def nrm(shape, scale):
        return jax.random.normal(next(ks), shape, f32) * scale

    def unif(shape, lo, hi):
        return jax.random.uniform(next(ks), shape, f32, lo, hi)

    def gain(shape):
        return 1.0 + nrm(shape, 0.02)

    D, H, N, F = D_MODEL, RWKV_HEADS, RWKV_HEAD_DIM, D_FF
    NA, NBL = N_A_LAYERS, N_B_LAYERS
    n_pages = PAST_LEN // PAGE_SIZE
    n_used = DEC_BATCH * n_pages
    n_pool = n_used + max(1, n_used // 4)
    x_prompt = nrm((BATCH, SEQ, D), 1.0)
    x_sample = nrm((DEC_BATCH, DEC_SEQ, D), 1.0)
    p_prompt = nrm((DEPTH, BATCH, SEQ, PLE_DIM), 1.0)
    p_sample = nrm((DEPTH, DEC_BATCH, DEC_SEQ, PLE_DIM), 1.0)
    state_wkv = nrm((NA, DEC_BATCH, H, N, N), 0.5)
    state_shift = nrm((NA, DEC_BATCH, D), 1.0)
    cache_k = nrm((n_pool, PAGE_SIZE, N_KV_HEADS, HEAD_DIM), 1.0)
    cache_v = nrm((n_pool, PAGE_SIZE, N_KV_HEADS, HEAD_DIM), 1.0)
    page_table = jax.random.permutation(next(ks), n_pool)[:n_used].reshape(DEC_BATCH, n_pages).astype(jnp.int32)
    return {
        'x_prompt': x_prompt,
        'x_sample': x_sample,
        'p_prompt': p_prompt,
        'p_sample': p_sample,
        'state_wkv': state_wkv,
        'state_shift': state_shift,
        'cache_k': cache_k,
        'cache_v': cache_v,
        'page_table': page_table,
        'norm_mix': gain((DEPTH, D)),
        'norm_ffn': gain((DEPTH, D)),
        'norm_ple': gain((DEPTH, D)),
        'norm_kv': gain((D,)),
        'norm_final': gain((D,)),
        'rwkv_mu': unif((NA, 6, D), 0.0, 1.0),
        'rwkv_w_rkv': nrm((NA, 3, D, D), D ** -0.5),
        'rwkv_w_o': nrm((NA, D, D), D ** -0.5),
        'rwkv_w0': unif((NA, D), -4.0, 1.0),
        'rwkv_w1': nrm((NA, D, DECAY_LORA), D ** -0.5),
        'rwkv_w2': nrm((NA, DECAY_LORA, D), 0.1 * DECAY_LORA ** -0.5),
        'rwkv_a0': nrm((NA, D), 0.5),
        'rwkv_a1': nrm((NA, D, ICLR_LORA), D ** -0.5),
        'rwkv_a2': nrm((NA, ICLR_LORA, D), 0.1 * ICLR_LORA ** -0.5),
        'rwkv_g1': nrm((NA, D, GATE_LORA), D ** -0.5),
        'rwkv_g2': nrm((NA, GATE_LORA, D), GATE_LORA ** -0.5),
        'rwkv_k_k': unif((NA, D), 0.7, 1.0),
        'rwkv_k_a': unif((NA, D), 0.8, 1.2),
        'rwkv_r_k': nrm((NA, H, N), 0.1),
        'rwkv_lnx_w': gain((NA, D)),
        'rwkv_lnx_b': nrm((NA, D), 0.02),
        'attn_w_q': nrm((NBL, D, N_HEADS * HEAD_DIM), D ** -0.5),
        'attn_w_o': nrm((NBL, N_HEADS * HEAD_DIM, D), D ** -0.5),
        'kv_w_k': nrm((D, N_KV_HEADS * HEAD_DIM), D ** -0.5),
        'kv_w_v': nrm((D, N_KV_HEADS * HEAD_DIM), D ** -0.5),
        'ffn_w_gate': nrm((DEPTH, D, F), D ** -0.5),
        'ffn_w_up': nrm((DEPTH, D, F), D ** -0.5),
        'ffn_w_down': nrm((DEPTH, F, D), F ** -0.5),
        'ple_w_proj': nrm((DEPTH, PLE_DIM, D), PLE_DIM ** -0.5),
        'ple_w_gate': nrm((DEPTH, D, D), D ** -0.5),
    }


def reference(x_prompt, x_sample, p_prompt, p_sample, state_wkv, state_shift, cache_k, cache_v,
              page_table, norm_mix, norm_ffn, norm_ple, norm_kv, norm_final,
              rwkv_mu, rwkv_w_rkv, rwkv_w_o, rwkv_w0, rwkv_w1, rwkv_w2, rwkv_a0, rwkv_a1, rwkv_a2,
              rwkv_g1, rwkv_g2, rwkv_k_k, rwkv_k_a, rwkv_r_k, rwkv_lnx_w, rwkv_lnx_b,
              attn_w_q, attn_w_o, kv_w_k, kv_w_v, ffn_w_gate, ffn_w_up, ffn_w_down,
              ple_w_proj, ple_w_gate):
    W = dict(norm_mix=norm_mix, norm_ffn=norm_ffn, norm_ple=norm_ple, norm_kv=norm_kv,
             norm_final=norm_final, rwkv_mu=rwkv_mu, rwkv_w_rkv=rwkv_w_rkv, rwkv_w_o=rwkv_w_o,
             rwkv_w0=rwkv_w0, rwkv_w1=rwkv_w1, rwkv_w2=rwkv_w2, rwkv_a0=rwkv_a0, rwkv_a1=rwkv_a1,
             rwkv_a2=rwkv_a2, rwkv_g1=rwkv_g1, rwkv_g2=rwkv_g2, rwkv_k_k=rwkv_k_k, rwkv_k_a=rwkv_k_a,
             rwkv_r_k=rwkv_r_k, rwkv_lnx_w=rwkv_lnx_w, rwkv_lnx_b=rwkv_lnx_b,
             attn_w_q=attn_w_q, attn_w_o=attn_w_o, kv_w_k=kv_w_k, kv_w_v=kv_w_v,
             ffn_w_gate=ffn_w_gate, ffn_w_up=ffn_w_up, ffn_w_down=ffn_w_down,
             ple_w_proj=ple_w_proj, ple_w_gate=ple_w_gate)
    b_p = x_prompt.shape[0]
    wkv0 = jnp.zeros((N_A_LAYERS, b_p, RWKV_HEADS, RWKV_HEAD_DIM, RWKV_HEAD_DIM), jnp.float32)
    shift0 = jnp.zeros((N_A_LAYERS, b_p, D_MODEL), x_prompt.dtype)
    y_prompt, wkv_prompt, shift_prompt, k_prompt, v_prompt = trunk(
        x_prompt, p_prompt, wkv0, shift0, None, None, 0, W)
    b_s = x_sample.shape[0]
    past_len = page_table.shape[1] * PAGE_SIZE
    past_k = cache_k[page_table].reshape(b_s, past_len, N_KV_HEADS, HEAD_DIM)
    past_v = cache_v[page_table].reshape(b_s, past_len, N_KV_HEADS, HEAD_DIM)
    y_sample, wkv_sample, shift_sample, k_sample, v_sample = trunk(
        x_sample, p_sample, state_wkv, state_shift, past_k, past_v, past_len, W)
    return (y_prompt, y_sample, wkv_prompt, shift_prompt, k_prompt, v_prompt,
            wkv_sample, shift_sample, k_sample, v_sample)
```

```python
import functools
import math

import jax
import jax.numpy as jnp
from jax import lax
from jax.experimental import pallas as pl
from jax.experimental.pallas import tpu as pltpu

F32, BF16 = jnp.float32, jnp.bfloat16
N_HEADS = 16
HEAD_DIM = 64
LANES = 128
MOBA_BLOCK = 256
MOBA_TOPK = 3
PAGE_SIZE = 128
RMS_EPS = 1e-6
LNX_EPS = 64e-5
NEG = -1e30
VMEM_LIMIT_BYTES = 56 * 1024 * 1024
SCAN_CHUNK = 64


def _cparams(*sem):
    return pltpu.CompilerParams(dimension_semantics=sem, vmem_limit_bytes=VMEM_LIMIT_BYTES)


def _rms(x, g):
    return x * lax.rsqrt(jnp.mean(x * x, axis=-1, keepdims=True) + RMS_EPS) * g


def _sigmoid(x):
    return 1.0 / (1.0 + jnp.exp(-x))


_NN = (((1,), (0,)), ((), ()))
_NT = (((1,), (1,)), ((), ()))


def _dot(a, b, dims=_NN):
    return lax.dot_general(a, b, dims, preferred_element_type=F32)


def _bdot(a, w):
    return _dot(a.astype(BF16), w)


def _split2(x):
    hi = x.astype(BF16)
    lo = (x - hi.astype(F32)).astype(BF16)
    return hi, lo


def _dot3(a, b, dims=_NN):
    ah, al = _split2(a)
    bh, bl = _split2(b)
    return _dot(ah, bh, dims) + (_dot(ah, bl, dims) + _dot(al, bh, dims))


def _full(shape):
    n = len(shape)
    return pl.BlockSpec(shape, lambda *_: (0,) * n)


def _rms_kernel(x_ref, g_ref, o_ref):
    o_ref[...] = _rms(x_ref[...], g_ref[...])


def rmsnorm_call(x, g, tm):
    t, d = x.shape
    row = pl.BlockSpec((tm, d), lambda i: (i, 0))
    return pl.pallas_call(
        _rms_kernel, grid=(t // tm,), in_specs=[row, _full((1, d))], out_specs=row,
        out_shape=jax.ShapeDtypeStruct((t, d), F32), compiler_params=_cparams("parallel"),
        name="rmsnorm")(x, g)


def _rwkv_proj_kernel(hn_ref, xp_ref, mu_ref, wr, wk, wv, w1, w2, a1, a2, g1, g2, w0, a0,
                      r_o, k_o, v_o, lw_o, a_o, g_o):
    hn = hn_ref[...]
    xx = xp_ref[...] - hn

    def mix(i):
        return hn + xx * mu_ref[i:i + 1, :]

    r_o[...] = _bdot(mix(0), wr[...])
    k_o[...] = _bdot(mix(2), wk[...])
    v_o[...] = _bdot(mix(3), wv[...])
    u = w0[...] + _bdot(jnp.tanh(_bdot(mix(1), w1[...])), w2[...])
    lw_o[...] = -math.exp(-0.5) * _sigmoid(u)
    a_o[...] = _sigmoid(a0[...] + _bdot(_bdot(mix(4), a1[...]), a2[...]))
    g_o[...] = _bdot(_sigmoid(_bdot(mix(5), g1[...])), g2[...])


def rwkv_proj_call(hn, xp, mu, wr, wk, wv, w1, w2, a1, a2, g1, g2, w0, a0, tm):
    t, d = hn.shape
    row = pl.BlockSpec((tm, d), lambda i: (i, 0))
    ws = [mu, wr, wk, wv, w1, w2, a1, a2, g1, g2, w0, a0]
    return pl.pallas_call(
        _rwkv_proj_kernel, grid=(t // tm,),
        in_specs=[row, row] + [_full(w.shape) for w in ws],
        out_specs=[row] * 6, out_shape=[jax.ShapeDtypeStruct((t, d), F32)] * 6,
        compiler_params=_cparams("parallel"), name="rwkv_proj")(hn, xp, *ws)


def _matmul_res_kernel(x_ref, w_ref, res_ref, o_ref):
    o_ref[...] = res_ref[...] + _bdot(x_ref[...], w_ref[...])


def matmul_res_call(x, w, res, tm):
    t, d = x.shape
    row = pl.BlockSpec((tm, d), lambda i: (i, 0))
    return pl.pallas_call(
        _matmul_res_kernel, grid=(t // tm,), in_specs=[row, _full(w.shape), row], out_specs=row,
        out_shape=jax.ShapeDtypeStruct((t, d), F32), compiler_params=_cparams("parallel"),
        name="matmul_res")(x, w, res)


def _ffn_kernel(h_ref, g_ref, wg_ref, wu_ref, wd_ref, o_ref, xn_scr):
    @pl.when(pl.program_id(1) == 0)
    def _():
        h = h_ref[...]
        xn_scr[...] = _rms(h, g_ref[...]).astype(BF16)
        o_ref[...] = h

    xn = xn_scr[...]
    gt = _dot(xn, wg_ref[...])
    up = _dot(xn, wu_ref[...])
    act = gt * _sigmoid(gt) * up
    o_ref[...] += _bdot(act, wd_ref[...])


def ffn_call(h, g, wg, wu, wd, tm, tf):
    t, d = h.shape
    f = wg.shape[1]
    row = pl.BlockSpec((tm, d), lambda i, j: (i, 0))
    return pl.pallas_call(
        _ffn_kernel, grid=(t // tm, f // tf),
        in_specs=[row, pl.BlockSpec((1, d), lambda i, j: (0, 0)),
                  pl.BlockSpec((d, tf), lambda i, j: (0, j)),
                  pl.BlockSpec((d, tf), lambda i, j: (0, j)),
                  pl.BlockSpec((tf, d), lambda i, j: (j, 0))],
        out_specs=row, out_shape=jax.ShapeDtypeStruct((t, d), F32),
        scratch_shapes=[pltpu.VMEM((tm, d), BF16)],
        compiler_params=_cparams("parallel", "arbitrary"), name="ffn")(h, g, wg, wu, wd)


def _ple_kernel(h_ref, p_ref, g_ref, wg_ref, wp_ref, *rest, final):
    h = h_ref[...]
    gate = _sigmoid(_bdot(_rms(h, g_ref[...]), wg_ref[...]))
    hn = h + _bdot(p_ref[...], wp_ref[...]) * gate
    if final:
        gf_ref, y_ref = rest
        y_ref[...] = _rms(hn, gf_ref[...])
    else:
        rest[0][...] = hn


def ple_call(h, p, g, wg, wp, tm, g_final=None):
    t, d = h.shape
    row = pl.BlockSpec((tm, d), lambda i: (i, 0))
    prow = pl.BlockSpec((tm, p.shape[1]), lambda i: (i, 0))
    ins = [h, p, g, wg, wp]
    specs = [row, prow, _full(g.shape), _full(wg.shape), _full(wp.shape)]
    if g_final is not None:
        ins.append(g_final)
        specs.append(_full(g_final.shape))
    return pl.pallas_call(
        functools.partial(_ple_kernel, final=g_final is not None), grid=(t // tm,),
        in_specs=specs, out_specs=row, out_shape=jax.ShapeDtypeStruct((t, d), F32),
        compiler_params=_cparams("parallel"), name="ple")(*ins)


def _kvq_kernel(h_ref, gkv_ref, gq_ref, wk_ref, wv_ref, wq_ref, *rest, prompt):
    h = h_ref[...]
    hk = _rms(h, gkv_ref[...]).astype(BF16)
    k = _dot(hk, wk_ref[...])
    if prompt:
        wvt_ref, k_o, v_o, q_o, kb_o, vt_o, km_o = rest
    else:
        k_o, v_o, q_o = rest
    k_o[...] = k
    v_o[...] = _dot(hk, wv_ref[...])
    q_o[...] = _bdot(_rms(h, gq_ref[...]), wq_ref[...])
    if prompt:
        kb_o[...] = k.astype(BF16)
        vt_o[...] = _dot(wvt_ref[...], hk, _NT).astype(BF16)
        km_o[...] = jnp.sum(k, axis=0, keepdims=True) * (1.0 / MOBA_BLOCK)


def kvq_call(h, gkv, gq, wk, wv, wq, tm, wvt=None, seq=None):
    t, d = h.shape
    prompt = wvt is not None
    row = pl.BlockSpec((tm, d), lambda i: (i, 0))
    ins = [h, gkv, gq, wk, wv, wq]
    specs = [row, _full(gkv.shape), _full(gq.shape), _full(wk.shape), _full(wv.shape), _full(wq.shape)]
    outs = [jax.ShapeDtypeStruct((t, d), F32)] * 3
    ospecs = [row] * 3
    if prompt:
        assert tm == MOBA_BLOCK
        tps = seq // tm
        ins.append(wvt)
        specs.append(_full(wvt.shape))
        outs += [jax.ShapeDtypeStruct((t, d), BF16),
                 jax.ShapeDtypeStruct((t // seq, d, seq), BF16),
                 jax.ShapeDtypeStruct((t // tm, 1, d), F32)]
        ospecs += [row, pl.BlockSpec((None, d, tm), lambda i: (i // tps, 0, i % tps)),
                   pl.BlockSpec((None, 1, d), lambda i: (i, 0, 0))]
    return pl.pallas_call(
        functools.partial(_kvq_kernel, prompt=prompt), grid=(t // tm,), in_specs=specs,
        out_specs=ospecs, out_shape=outs, compiler_params=_cparams("parallel"), name="kvq")(*ins)


def _scan_kernel(r_ref, k_ref, v_ref, lw_ref, a_ref, g_ref, kk_ref, ka_ref, rk_ref, lnw_ref, lnb_ref,
                 z0_ref, o_ref, zout_ref, z_scr, *, C):
    c = pl.program_id(2)

    @pl.when(c == 0)
    def _():
        z_scr[...] = z0_ref[...]

    lane = lax.broadcasted_iota(jnp.int32, (1, LANES), 1)
    m0 = lane < HEAD_DIM
    masks = (m0, jnp.logical_not(m0))

    def hsum(x):
        s0 = jnp.sum(jnp.where(m0, x, 0.0), axis=-1, keepdims=True)
        s1 = jnp.sum(jnp.where(m0, 0.0, x), axis=-1, keepdims=True)
        return jnp.where(m0, s0, s1)

    r, k, v, lw, a = r_ref[...], k_ref[...], v_ref[...], lw_ref[...], a_ref[...]
    kkr = k * kk_ref[...]
    kk = kkr / jnp.maximum(jnp.sqrt(hsum(kkr * kkr)), 1e-12)
    beta = kk * a
    kmod = k * (1.0 + (a - 1.0) * ka_ref[...])

    ri = lax.broadcasted_iota(jnp.int32, (C, C), 0)
    ci = lax.broadcasted_iota(jnp.int32, (C, C), 1)
    ltri = jnp.where(ci <= ri, 1.0, 0.0).astype(BF16)
    l1 = lw.astype(BF16)
    rem = lw - l1.astype(F32)
    l2 = rem.astype(BF16)
    l3 = (rem - l2.astype(F32)).astype(BF16)
    cl = _dot(ltri, l1) + (_dot(ltri, l2) + _dot(ltri, l3))
    clc = cl[C - 1:C, :]

    gam = jnp.exp(cl)
    igam = jnp.exp(-cl)
    a_t = -kk * jnp.exp(cl - lw)
    r_t = r * gam
    b_h = beta * igam
    k_h = kmod * igam
    gout = jnp.exp(clc - cl)
    bk_t = jnp.concatenate([(beta * gout).T, (kmod * gout).T], axis=1)
    gc_col = jnp.exp(jnp.sum(lw.T, axis=1, keepdims=True))

    z = z_scr[...]
    ar = jnp.concatenate([a_t, r_t], axis=0)
    xs = _dot3(ar, z)
    rhs_cols = jnp.concatenate([b_h, b_h, k_h], axis=0)

    strict_c = ci < ri
    incl_c = ci <= ri
    rw = lax.broadcasted_iota(jnp.int32, (C, 2 * C), 0)
    cw = lax.broadcasted_iota(jnp.int32, (C, 2 * C), 1)
    strict_w = jnp.where(cw < C, cw, cw - C) < rw
    right_w = cw >= C
    eye_w = cw - C == rw
    zero_c = jnp.zeros((C, LANES), F32)

    gm, av = [], []
    for h in range(2):
        gfull = _dot3(jnp.where(masks[h], ar, 0.0), rhs_cols, _NT)
        w = jnp.where(strict_w, gfull[:C, :2 * C], 0.0)
        gk = jnp.concatenate([jnp.where(strict_c, gfull[:C, 2 * C:], 0.0),
                              jnp.where(incl_c, gfull[C:, 2 * C:], 0.0)], axis=0)
        arb = jnp.where(incl_c, gfull[C:, :C], 0.0)
        for _ in range(int(math.log2(C))):
            w = jnp.where(right_w, w, 0.0) + _dot3(w[:, :C], w)
        minv = jnp.where(right_w, w, 0.0) + jnp.where(eye_w, 1.0, 0.0)
        gm.append((minv, arb))
        av.append(_dot3(gk, v))
    avs = jnp.where(m0, av[0], av[1])

    rhs = jnp.concatenate([zero_c, xs[:C] + avs[:C]], axis=0)
    u = jnp.where(m0, _dot3(gm[0][0], rhs), _dot3(gm[1][0], rhs))
    o = xs[C:] + avs[C:] + jnp.where(m0, _dot3(gm[0][1], u), _dot3(gm[1][1], u))

    uv = jnp.concatenate([u, v], axis=0)
    rz = lax.broadcasted_iota(jnp.int32, (LANES, LANES), 0)
    cz = lax.broadcasted_iota(jnp.int32, (LANES, LANES), 1)
    same_head = (rz < HEAD_DIM) == (cz < HEAD_DIM)
    z_new = jnp.where(same_head, z * gc_col + _dot3(bk_t, uv), 0.0)
    z_scr[...] = z_new

    @pl.when(c == pl.num_programs(2) - 1)
    def _():
        zout_ref[...] = z_new

    inv_n = 1.0 / HEAD_DIM
    mean = hsum(o) * inv_n
    dlt = o - mean
    var = hsum(dlt * dlt) * inv_n
    on = dlt * lax.rsqrt(var + LNX_EPS) * lnw_ref[...] + lnb_ref[...]
    bonus = hsum(r * kmod * rk_ref[...]) * v
    o_ref[...] = (on + bonus) * g_ref[...]


def scan_call(r, k, v, lw, a, g, k_k, k_a, r_k, lnw, lnb, z0, nb, seq, C):
    t, d = r.shape
    npairs = d // LANES
    nc = seq // C
    row = pl.BlockSpec((C, LANES), lambda b, p, c: (b * nc + c, p))
    par = pl.BlockSpec((1, LANES), lambda b, p, c: (0, p))
    zspec = pl.BlockSpec((None, None, LANES, LANES), lambda b, p, c: (b, p, 0, 0))
    return pl.pallas_call(
        functools.partial(_scan_kernel, C=C), grid=(nb, npairs, nc),
        in_specs=[row] * 6 + [par] * 5 + [zspec],
        out_specs=[row, zspec],
        out_shape=[jax.ShapeDtypeStruct((t, d), F32),
                   jax.ShapeDtypeStruct((nb, npairs, LANES, LANES), F32)],
        scratch_shapes=[pltpu.VMEM((LANES, LANES), F32)],
        compiler_params=_cparams("parallel", "parallel", "arbitrary"), name="rwkv_scan")(
            r, k, v, lw, a, g, k_k, k_a, r_k, lnw, lnb, z0)


def _state_to_z(s):
    b, h = s.shape[:2]
    st = jnp.swapaxes(s, 2, 3).reshape(b, h // 2, 2, HEAD_DIM, HEAD_DIM)
    z = jnp.zeros((b, h // 2, 2, HEAD_DIM, 2, HEAD_DIM), F32)
    z = z.at[:, :, 0, :, 0, :].set(st[:, :, 0]).at[:, :, 1, :, 1, :].set(st[:, :, 1])
    return z.reshape(b, h // 2, LANES, LANES)


def _z_to_state(z):
    b, hp = z.shape[:2]
    z6 = z.reshape(b, hp, 2, HEAD_DIM, 2, HEAD_DIM)
    st = jnp.stack([z6[:, :, 0, :, 0, :], z6[:, :, 1, :, 1, :]], axis=2)
    return jnp.swapaxes(st.reshape(b, hp * 2, HEAD_DIM, HEAD_DIM), 2, 3)


def _top3_rows(gate, rowi, nrow):
    sel = jnp.zeros(gate.shape, jnp.bool_)
    for _ in range(MOBA_TOPK):
        mx = jnp.max(gate, axis=0, keepdims=True)
        idx = jnp.min(jnp.where(gate == mx, rowi, nrow), axis=0, keepdims=True)
        pick = jnp.logical_and(rowi == idx, mx > -jnp.inf)
        sel = jnp.logical_or(sel, pick)
        gate = jnp.where(pick, -jnp.inf, gate)
    return sel


def _moba_prompt_kernel(sl_ref, q_ref, k_ref, vt_ref, km_ref, o_ref, sel_scr, *, nb):
    p = pl.program_id(1)
    i = pl.program_id(2)
    blk = MOBA_BLOCK
    lane = lax.broadcasted_iota(jnp.int32, (1, LANES), 1)
    m0 = lane < HEAD_DIM
    masks = (m0, jnp.logical_not(m0))
    q = q_ref[...]
    km = km_ref[...]
    rowi = lax.broadcasted_iota(jnp.int32, (nb, blk), 0)
    dist0 = (lax.broadcasted_iota(jnp.int32, (blk, blk), 1)
             - lax.broadcasted_iota(jnp.int32, (blk, blk), 0)).astype(F32)
    scale = 1.0 / math.sqrt(HEAD_DIM)

    qb, slopes, ebias = [], [], []
    for h in range(2):
        qh = jnp.where(masks[h], q, 0.0)
        gate = jnp.where(rowi < i, _dot3(km, qh, _NT), -jnp.inf)
        sel_scr[h] = jnp.where(_top3_rows(gate, rowi, nb), 0.0, NEG)
        qb.append((qh * scale).astype(BF16))
        slope = sl_ref[2 * p + h]
        slopes.append(slope)
        ebias.append(dist0 * (-slope))

    def attend(kt, vt, biases, carry):
        new = []
        for h in range(2):
            m, l, acc = carry[h]
            s = _dot(kt, qb[h], _NT) + biases[h]
            m_new = jnp.maximum(m, jnp.max(s, axis=0, keepdims=True))
            alpha = jnp.exp(m - m_new)
            pr = jnp.exp(s - m_new)
            l = alpha * l + jnp.sum(pr, axis=0, keepdims=True)
            acc = alpha * acc + _dot(vt[h * HEAD_DIM:(h + 1) * HEAD_DIM, :], pr.astype(BF16))
            new.append((m_new, l, acc))
        return tuple(new)

    own = pl.multiple_of(i * blk, blk)
    init = tuple((jnp.full((1, blk), NEG, F32), jnp.zeros((1, blk), F32),
                  jnp.zeros((HEAD_DIM, blk), F32)) for _ in range(2))
    carry = attend(k_ref[pl.ds(own, blk), :], vt_ref[:, pl.ds(own, blk)],
                   [jnp.where(dist0 >= 0, ebias[h], NEG) for h in range(2)], init)

    def body(j, carry):
        off = pl.multiple_of(j * blk, blk)
        gap = ((i - j) * blk).astype(F32)
        biases = [ebias[h] + (sel_scr[h, pl.ds(j, 1), :] - slopes[h] * gap) for h in range(2)]
        return attend(k_ref[pl.ds(off, blk), :], vt_ref[:, pl.ds(off, blk)], biases, carry)

    carry = lax.fori_loop(0, i, body, carry)
    out_t = jnp.concatenate([carry[h][2] / carry[h][1] for h in range(2)], axis=0)
    o_ref[...] = out_t.T


def moba_prompt_call(slopes, q, kb, vt, kmean, nbatch, seq):
    t, d = q.shape
    nb = seq // MOBA_BLOCK
    npairs = d // LANES
    qspec = pl.BlockSpec((MOBA_BLOCK, LANES), lambda b, p, i: (b * nb + i, p))
    return pl.pallas_call(
        functools.partial(_moba_prompt_kernel, nb=nb), grid=(nbatch, npairs, nb),
        in_specs=[pl.BlockSpec(memory_space=pltpu.SMEM), qspec,
                  pl.BlockSpec((seq, LANES), lambda b, p, i: (b, p)),
                  pl.BlockSpec((None, LANES, seq), lambda b, p, i: (b, p, 0)),
                  pl.BlockSpec((None, nb, LANES), lambda b, p, i: (b, 0, p))],
        out_specs=qspec, out_shape=jax.ShapeDtypeStruct((t, d), F32),
        scratch_shapes=[pltpu.VMEM((2, nb, MOBA_BLOCK), F32)],
        compiler_params=_cparams("parallel", "parallel", "arbitrary"), name="moba_prompt")(
            slopes, q, kb, vt, kmean)


def _sample_scores_kernel(pt_ref, q_ref, ka_ref, kb_ref, sc_ref, g_ref):
    del pt_ref
    j = pl.program_id(1)
    kblk = jnp.concatenate([ka_ref[...], kb_ref[...]], axis=0)
    qbd = q_ref[...]
    sc_ref[...] = _dot(qbd.astype(BF16), kblk.astype(BF16), _NT)
    kmean = jnp.sum(kblk, axis=0, keepdims=True) * (1.0 / MOBA_BLOCK)
    gcol = jnp.sum(qbd * kmean, axis=1, keepdims=True)
    lane = lax.broadcasted_iota(jnp.int32, g_ref.shape, 1)

    @pl.when(j == 0)
    def _():
        g_ref[...] = jnp.zeros(g_ref.shape, F32)

    g_ref[...] = jnp.where(lane == j, gcol, g_ref[...])


def sample_scores_call(page_table, qbd, cache_k):
    bs, rows, d = qbd.shape
    npg = page_table.shape[1]
    nbp = npg * PAGE_SIZE // MOBA_BLOCK
    ppb = MOBA_BLOCK // PAGE_SIZE
    assert ppb == 2
    page = lambda o: pl.BlockSpec((None, PAGE_SIZE, d), lambda b, j, pt: (pt[b * npg + ppb * j + o], 0, 0))
    gs = pltpu.PrefetchScalarGridSpec(
        num_scalar_prefetch=1, grid=(bs, nbp),
        in_specs=[pl.BlockSpec((None, rows, d), lambda b, j, pt: (b, 0, 0)), page(0), page(1)],
        out_specs=[pl.BlockSpec((None, rows, MOBA_BLOCK), lambda b, j, pt: (b, 0, j)),
                   pl.BlockSpec((None, rows, nbp), lambda b, j, pt: (b, 0, 0))])
    return pl.pallas_call(
        _sample_scores_kernel, grid_spec=gs,
        out_shape=[jax.ShapeDtypeStruct((bs, rows, nbp * MOBA_BLOCK), F32),
                   jax.ShapeDtypeStruct((bs, rows, nbp), F32)],
        compiler_params=_cparams("parallel", "arbitrary"), name="sample_scores")(
            page_table.reshape(-1), qbd, cache_k, cache_k)


def _sample_attend_kernel(pt_ref, sc_ref, g_ref, q_ref, ko_ref, vo_ref, sl_ref, va_ref, vb_ref, o_ref,
                          m_scr, l_scr, acc_scr, sel_scr, *, nbp, tq, past_len):
    del pt_ref
    j = pl.program_id(1)
    rows, d = acc_scr.shape
    scale = 1.0 / math.sqrt(HEAD_DIM)
    rowc = lax.broadcasted_iota(jnp.int32, (rows, 1), 0)
    tcol = rowc // N_HEADS
    slope = sl_ref[...]

    @pl.when(j == 0)
    def _():
        gate = g_ref[...]
        lane = lax.broadcasted_iota(jnp.int32, gate.shape, 1)
        sel = jnp.zeros(gate.shape, jnp.bool_)
        for _ in range(MOBA_TOPK):
            mx = jnp.max(gate, axis=1, keepdims=True)
            idx = jnp.min(jnp.where(gate == mx, lane, nbp), axis=1, keepdims=True)
            pick = jnp.logical_and(lane == idx, mx > -jnp.inf)
            sel = jnp.logical_or(sel, pick)
            gate = jnp.where(pick, -jnp.inf, gate)
        sel_scr[...] = jnp.where(sel, 0.0, NEG)
        ko = ko_ref[...]
        s = _dot(q_ref[...].astype(BF16), ko.astype(BF16), _NT) * scale
        tk = lax.broadcasted_iota(jnp.int32, s.shape, 1)
        s = jnp.where(tk <= tcol, s - slope * (tcol - tk).astype(F32), NEG)
        m = jnp.max(s, axis=1, keepdims=True)
        pr = jnp.exp(s - m)
        m_scr[...] = m
        l_scr[...] = jnp.sum(pr, axis=1, keepdims=True)
        acc_scr[...] = _dot(pr.astype(BF16), vo_ref[...].astype(BF16))

    lane_k = lax.broadcasted_iota(jnp.int32, (rows, MOBA_BLOCK), 1)
    dist = (past_len + tcol) - (j * MOBA_BLOCK + lane_k)
    lane_b = lax.broadcasted_iota(jnp.int32, (rows, nbp), 1)
    bias = jnp.sum(jnp.where(lane_b == j, sel_scr[...], 0.0), axis=1, keepdims=True)
    s = sc_ref[...] * scale - slope * dist.astype(F32) + bias
    m_old = m_scr[...]
    m_new = jnp.maximum(m_old, jnp.max(s, axis=1, keepdims=True))
    alpha = jnp.exp(m_old - m_new)
    pr = jnp.exp(s - m_new)
    vblk = jnp.concatenate([va_ref[...], vb_ref[...]], axis=0).astype(BF16)
    m_scr[...] = m_new
    l_scr[...] = alpha * l_scr[...] + jnp.sum(pr, axis=1, keepdims=True)
    acc_scr[...] = alpha * acc_scr[...] + _dot(pr.astype(BF16), vblk)

    @pl.when(j == nbp - 1)
    def _():
        lane_d = lax.broadcasted_iota(jnp.int32, (rows, d), 1)
        own_head = lane_d // HEAD_DIM == rowc % N_HEADS
        res = jnp.where(own_head, acc_scr[...] / l_scr[...], 0.0)
        for t in range(tq):
            o_ref[t:t + 1, :] = jnp.sum(res[t * N_HEADS:(t + 1) * N_HEADS], axis=0, keepdims=True)


def sample_attend_call(page_table, scores, gate, qbd, k_own, v_own, slope_col, cache_v, tq):
    bs, rows, d = qbd.shape
    npg = page_table.shape[1]
    nbp = npg * PAGE_SIZE // MOBA_BLOCK
    ppb = MOBA_BLOCK // PAGE_SIZE
    page = lambda o: pl.BlockSpec((None, PAGE_SIZE, d), lambda b, j, pt: (pt[b * npg + ppb * j + o], 0, 0))
    per_b = lambda shape: pl.BlockSpec((None,) + shape, lambda b, j, pt: (b, 0, 0))
    gs = pltpu.PrefetchScalarGridSpec(
        num_scalar_prefetch=1, grid=(bs, nbp),
        in_specs=[pl.BlockSpec((None, rows, MOBA_BLOCK), lambda b, j, pt: (b, 0, j)),
                  per_b((rows, nbp)), per_b((rows, d)), per_b(k_own.shape[1:]), per_b(v_own.shape[1:]),
                  pl.BlockSpec((rows, 1), lambda b, j, pt: (0, 0)), page(0), page(1)],
        out_specs=per_b((tq, d)),
        scratch_shapes=[pltpu.VMEM((rows, 1), F32), pltpu.VMEM((rows, 1), F32),
                        pltpu.VMEM((rows, d), F32), pltpu.VMEM((rows, nbp), F32)])
    return pl.pallas_call(
        functools.partial(_sample_attend_kernel, nbp=nbp, tq=tq, past_len=npg * PAGE_SIZE),
        grid_spec=gs, out_shape=jax.ShapeDtypeStruct((bs, tq, d), F32),
        compiler_params=_cparams("parallel", "arbitrary"), name="sample_attend")(
            page_table.reshape(-1), scores, gate, qbd, k_own, v_own, slope_col, cache_v, cache_v)


def _row_tile(t):
    for tm in (512, 256, 128):
        if t % tm == 0:
            return tm
    raise ValueError(f"token count {t} is not a multiple of 128")


def _ffn_tile(f):
    for nf in (1, 2, 4, 11, 22):
        if f % nf == 0 and (f // nf) % LANES == 0 and f // nf <= 1536:
            return f // nf
    raise ValueError(f"unsupported FFN width {f}")


def _trunk(x, p, wkv_in, shift_in, W, cache=None):
    b, t, d = x.shape
    nt = b * t
    tm = _row_tile(nt)
    h = x.reshape(nt, d)
    row = lambda v: v.reshape(1, -1)

    hn = rmsnorm_call(h, row(W['norm_mix'][0]), tm)
    hn3 = hn.reshape(b, t, d)
    xp = jnp.concatenate([shift_in[:, None, :], hn3[:, :-1]], axis=1).reshape(nt, d)
    shift_out = hn3[:, -1]
    r, k, v, lw, a, g = rwkv_proj_call(
        hn, xp, W['rwkv_mu'][0], W['w_r'], W['w_k'], W['w_v'], W['rwkv_w1'], W['rwkv_w2'],
        W['rwkv_a1'], W['rwkv_a2'], W['rwkv_g1'], W['rwkv_g2'],
        row(W['rwkv_w0'][0]), row(W['rwkv_a0'][0]), tm)
    chunk = SCAN_CHUNK if t % SCAN_CHUNK == 0 else 8
    tp = -(-t // chunk) * chunk
    seqs = (r, k, v, lw, a, g)
    if tp != t:
        seqs = tuple(jnp.pad(s.reshape(b, t, d), ((0, 0), (0, tp - t), (0, 0))).reshape(b * tp, d)
                     for s in seqs)
    o, zf = scan_call(*seqs, row(W['rwkv_k_k'][0]), row(W['rwkv_k_a'][0]), row(W['rwkv_r_k'][0]),
                      row(W['rwkv_lnx_w'][0]), row(W['rwkv_lnx_b'][0]), _state_to_z(wkv_in), b, tp, chunk)
    if tp != t:
        o = o.reshape(b, tp, d)[:, :t].reshape(nt, d)
    wkv_out = _z_to_state(zf)
    h = matmul_res_call(o, W['rwkv_w_o'], h, tm)
    tf = _ffn_tile(W['ffn_w_gate'][0].shape[1])
    h = ffn_call(h, row(W['norm_ffn'][0]), W['ffn_w_gate'][0], W['ffn_w_up'][0], W['ffn_w_down'][0], tm, tf)
    h = ple_call(h, p[0].reshape(nt, -1), row(W['norm_ple'][0]), W['ple_w_gate'][0], W['ple_w_proj'][0], tm)

    slopes = jnp.exp2(-8.0 * jnp.arange(1, N_HEADS + 1, dtype=F32) / N_HEADS)
    if cache is None:
        k_new, v_new, q, kb, vt, kmean = kvq_call(
            h, row(W['norm_kv']), row(W['norm_mix'][1]), W['kv_w_k'], W['kv_w_v'], W['attn_w_q'],
            MOBA_BLOCK, wvt=W['kv_w_vT'], seq=t)
        attn = moba_prompt_call(slopes, q, kb, vt, kmean.reshape(b, t // MOBA_BLOCK, d), b, t)
    else:
        cache_k, cache_v, page_table = cache
        k_new, v_new, q = kvq_call(h, row(W['norm_kv']), row(W['norm_mix'][1]),
                                   W['kv_w_k'], W['kv_w_v'], W['attn_w_q'], tm)
        head_of_lane = jnp.arange(d) // HEAD_DIM
        hmask = (head_of_lane[None, :] == jnp.arange(N_HEADS)[:, None]).astype(F32)
        qbd = (q.reshape(b, t, 1, d) * hmask[None, None]).reshape(b, t * N_HEADS, d)
        pad8 = lambda z: jnp.pad(z.reshape(b, t, d), ((0, 0), (0, 8 - t), (0, 0)))
        slope_col = jnp.tile(slopes, t).reshape(t * N_HEADS, 1)
        npool = cache_k.shape[0]
        ck = cache_k.reshape(npool, PAGE_SIZE, d)
        cv = cache_v.reshape(npool, PAGE_SIZE, d)
        scores, gate = sample_scores_call(page_table, qbd, ck)
        attn = sample_attend_call(page_table, scores, gate, qbd, pad8(k_new), pad8(v_new),
                                  slope_col, cv, t).reshape(nt, d)
    h = matmul_res_call(attn, W['attn_w_o'], h, tm)
    h = ffn_call(h, row(W['norm_ffn'][1]), W['ffn_w_gate'][1], W['ffn_w_up'][1], W['ffn_w_down'][1], tm, tf)
    y = ple_call(h, p[1].reshape(nt, -1), row(W['norm_ple'][1]), W['ple_w_gate'][1], W['ple_w_proj'][1], tm,
                 g_final=row(W['norm_final']))
    shp = (b, t, N_HEADS, HEAD_DIM)
    return (y.reshape(b, t, d), wkv_out[None], shift_out[None], k_new.reshape(shp), v_new.reshape(shp))


def kernel(x_prompt, x_sample, p_prompt, p_sample, state_wkv, state_shift, cache_k, cache_v, page_table,
           norm_mix, norm_ffn, norm_ple, norm_kv, norm_final, rwkv_mu, rwkv_w_rkv, rwkv_w_o, rwkv_w0,
           rwkv_w1, rwkv_w2, rwkv_a0, rwkv_a1, rwkv_a2, rwkv_g1, rwkv_g2, rwkv_k_k, rwkv_k_a, rwkv_r_k,
           rwkv_lnx_w, rwkv_lnx_b, attn_w_q, attn_w_o, kv_w_k, kv_w_v, ffn_w_gate, ffn_w_up, ffn_w_down,
           ple_w_proj, ple_w_gate):
    assert norm_mix.shape[0] == 2 and state_wkv.shape[0] == 1, "one RWKV layer then one MoBA layer"
    bf = lambda w: w.astype(BF16)
    W = dict(norm_mix=norm_mix, norm_ffn=norm_ffn, norm_ple=norm_ple, norm_kv=norm_kv, norm_final=norm_final,
             rwkv_mu=rwkv_mu, w_r=bf(rwkv_w_rkv[0, 0]), w_k=bf(rwkv_w_rkv[0, 1]), w_v=bf(rwkv_w_rkv[0, 2]),
             rwkv_w_o=bf(rwkv_w_o[0]), rwkv_w0=rwkv_w0, rwkv_w1=bf(rwkv_w1[0]), rwkv_w2=bf(rwkv_w2[0]),
             rwkv_a0=rwkv_a0, rwkv_a1=bf(rwkv_a1[0]), rwkv_a2=bf(rwkv_a2[0]), rwkv_g1=bf(rwkv_g1[0]),
             rwkv_g2=bf(rwkv_g2[0]), rwkv_k_k=rwkv_k_k, rwkv_k_a=rwkv_k_a, rwkv_r_k=rwkv_r_k,
             rwkv_lnx_w=rwkv_lnx_w, rwkv_lnx_b=rwkv_lnx_b, attn_w_q=bf(attn_w_q[0]), attn_w_o=bf(attn_w_o[0]),
             kv_w_k=bf(kv_w_k), kv_w_v=bf(kv_w_v), kv_w_vT=bf(kv_w_v.T),
             ffn_w_gate=bf(ffn_w_gate), ffn_w_up=bf(ffn_w_up), ffn_w_down=bf(ffn_w_down),
             ple_w_proj=bf(ple_w_proj), ple_w_gate=bf(ple_w_gate))
    bp = x_prompt.shape[0]
    d = x_prompt.shape[2]
    wkv0 = jnp.zeros((bp, N_HEADS, HEAD_DIM, HEAD_DIM), F32)
    shift0 = jnp.zeros((bp, d), F32)
    y_p, wkv_p, shift_p, k_p, v_p = _trunk(x_prompt, p_prompt, wkv0, shift0, W)
    y_s, wkv_s, shift_s, k_s, v_s = _trunk(x_sample, p_sample, state_wkv[0], state_shift[0], W,
                                           cache=(cache_k, cache_v, page_table))
    return (y_p, y_s, wkv_p, shift_p, k_p, v_p, wkv_s, shift_s, k_s, v_s)
```

```python
import functools
import math

import jax
import jax.numpy as jnp
from jax import lax
from jax.experimental import pallas as pl
from jax.experimental.pallas import tpu as pltpu

F32, BF16 = jnp.float32, jnp.bfloat16
N_HEADS = 16
HEAD_DIM = 64
LANES = 128
MOBA_BLOCK = 256
MOBA_TOPK = 3
PAGE_SIZE = 128
RMS_EPS = 1e-6
LNX_EPS = 64e-5
NEG = -1e30
VMEM_LIMIT_BYTES = 56 * 1024 * 1024
SCAN_CHUNK = 64


def _cparams(*sem):
    return pltpu.CompilerParams(dimension_semantics=sem, vmem_limit_bytes=VMEM_LIMIT_BYTES)


def _rms(x, g):
    return x * lax.rsqrt(jnp.mean(x * x, axis=-1, keepdims=True) + RMS_EPS) * g


def _sigmoid(x):
    return 1.0 / (1.0 + jnp.exp(-x))


_NN = (((1,), (0,)), ((), ()))
_NT = (((1,), (1,)), ((), ()))


def _dot(a, b, dims=_NN):
    return lax.dot_general(a, b, dims, preferred_element_type=F32)


def _bf(x):
    return x.astype(BF16)


def _bdot(a, w):
    return _dot(a.astype(BF16), w)


def _split2(x):
    hi = x.astype(BF16)
    lo = (x - hi.astype(F32)).astype(BF16)
    return hi, lo


def _dot3(a, b, dims=_NN):
    ah, al = _split2(a)
    bh, bl = _split2(b)
    return _dot(ah, bh, dims) + (_dot(ah, bl, dims) + _dot(al, bh, dims))


def _full(shape):
    n = len(shape)
    return pl.BlockSpec(shape, lambda *_: (0,) * n)


def _rms_kernel(x_ref, g_ref, o_ref):
    o_ref[...] = _rms(x_ref[...], g_ref[...])


def rmsnorm_call(x, g, tm):
    t, d = x.shape
    row = pl.BlockSpec((tm, d), lambda i: (i, 0))
    return pl.pallas_call(
        _rms_kernel, grid=(t // tm,), in_specs=[row, _full((1, d))], out_specs=row,
        out_shape=jax.ShapeDtypeStruct((t, d), F32), compiler_params=_cparams("parallel"),
        name="rmsnorm")(x, g)


def _rwkv_proj_kernel(hn_ref, xp_ref, mu_ref, wr, wk, wv, w1, w2, a1, a2, g1, g2, w0, a0,
                      r_o, k_o, v_o, lw_o, a_o, g_o):
    hn = hn_ref[...]
    xx = xp_ref[...] - hn

    def mix(i):
        return hn + xx * mu_ref[i:i + 1, :]

    r_o[...] = _bdot(mix(0), wr[...])
    k_o[...] = _bdot(mix(2), wk[...])
    v_o[...] = _bdot(mix(3), wv[...])
    u = w0[...] + _bdot(jnp.tanh(_bdot(mix(1), w1[...])), w2[...])
    lw_o[...] = -math.exp(-0.5) * _sigmoid(u)
    a_o[...] = _sigmoid(a0[...] + _bdot(_bdot(mix(4), a1[...]), a2[...]))
    g_o[...] = _bdot(_sigmoid(_bdot(mix(5), g1[...])), g2[...])


def rwkv_proj_call(hn, xp, mu, wr, wk, wv, w1, w2, a1, a2, g1, g2, w0, a0, tm):
    t, d = hn.shape
    row = pl.BlockSpec((tm, d), lambda i: (i, 0))
    ws = [mu, wr, wk, wv, w1, w2, a1, a2, g1, g2, w0, a0]
    return pl.pallas_call(
        _rwkv_proj_kernel, grid=(t // tm,),
        in_specs=[row, row] + [_full(w.shape) for w in ws],
        out_specs=[row] * 6, out_shape=[jax.ShapeDtypeStruct((t, d), F32)] * 6,
        compiler_params=_cparams("parallel"), name="rwkv_proj")(hn, xp, *ws)


def _matmul_res_kernel(x_ref, w_ref, res_ref, o_ref):
    o_ref[...] = res_ref[...] + _bdot(x_ref[...], w_ref[...])


def matmul_res_call(x, w, res, tm):
    t, d = x.shape
    row = pl.BlockSpec((tm, d), lambda i: (i, 0))
    return pl.pallas_call(
        _matmul_res_kernel, grid=(t // tm,), in_specs=[row, _full(w.shape), row], out_specs=row,
        out_shape=jax.ShapeDtypeStruct((t, d), F32), compiler_params=_cparams("parallel"),
        name="matmul_res")(x, w, res)


def _ffn_kernel(h_ref, g_ref, wg_ref, wu_ref, wd_ref, o_ref, xn_scr):
    @pl.when(pl.program_id(1) == 0)
    def _():
        h = h_ref[...]
        xn_scr[...] = _rms(h, g_ref[...]).astype(BF16)
        o_ref[...] = h

    xn = xn_scr[...]
    gt = _dot(xn, wg_ref[...])
    up = _dot(xn, wu_ref[...])
    act = gt * _sigmoid(gt) * up
    o_ref[...] += _bdot(act, wd_ref[...])


def ffn_call(h, g, wg, wu, wd, tm, tf):
    t, d = h.shape
    f = wg.shape[1]
    row = pl.BlockSpec((tm, d), lambda i, j: (i, 0))
    return pl.pallas_call(
        _ffn_kernel, grid=(t // tm, f // tf),
        in_specs=[row, pl.BlockSpec((1, d), lambda i, j: (0, 0)),
                  pl.BlockSpec((d, tf), lambda i, j: (0, j)),
                  pl.BlockSpec((d, tf), lambda i, j: (0, j)),
                  pl.BlockSpec((tf, d), lambda i, j: (j, 0))],
        out_specs=row, out_shape=jax.ShapeDtypeStruct((t, d), F32),
        scratch_shapes=[pltpu.VMEM((tm, d), BF16)],
        compiler_params=_cparams("parallel", "arbitrary"), name="ffn")(h, g, wg, wu, wd)


def _ple_kernel(h_ref, p_ref, g_ref, wg_ref, wp_ref, *rest, final):
    h = h_ref[...]
    gate = _sigmoid(_bdot(_rms(h, g_ref[...]), wg_ref[...]))
    hn = h + _bdot(p_ref[...], wp_ref[...]) * gate
    if final:
        gf_ref, y_ref = rest
        y_ref[...] = _rms(hn, gf_ref[...])
    else:
        rest[0][...] = hn


def ple_call(h, p, g, wg, wp, tm, g_final=None):
    t, d = h.shape
    row = pl.BlockSpec((tm, d), lambda i: (i, 0))
    prow = pl.BlockSpec((tm, p.shape[1]), lambda i: (i, 0))
    ins = [h, p, g, wg, wp]
    specs = [row, prow, _full(g.shape), _full(wg.shape), _full(wp.shape)]
    if g_final is not None:
        ins.append(g_final)
        specs.append(_full(g_final.shape))
    return pl.pallas_call(
        functools.partial(_ple_kernel, final=g_final is not None), grid=(t // tm,),
        in_specs=specs, out_specs=row, out_shape=jax.ShapeDtypeStruct((t, d), F32),
        compiler_params=_cparams("parallel"), name="ple")(*ins)


def _kvq_kernel(h_ref, gkv_ref, gq_ref, wk_ref, wv_ref, wq_ref, *rest, prompt):
    h = h_ref[...]
    hk = _rms(h, gkv_ref[...]).astype(BF16)
    k = _dot(hk, wk_ref[...])
    if prompt:
        wvt_ref, k_o, v_o, q_o, kb_o, vt_o, km_o = rest
    else:
        k_o, v_o, q_o = rest
    k_o[...] = k
    v_o[...] = _dot(hk, wv_ref[...])
    q_o[...] = _bdot(_rms(h, gq_ref[...]), wq_ref[...])
    if prompt:
        kb_o[...] = k.astype(BF16)
        vt_o[...] = _dot(wvt_ref[...], hk, _NT).astype(BF16)
        km_o[...] = jnp.sum(k, axis=0, keepdims=True) * (1.0 / MOBA_BLOCK)


def kvq_call(h, gkv, gq, wk, wv, wq, tm, wvt=None, seq=None):
    t, d = h.shape
    prompt = wvt is not None
    row = pl.BlockSpec((tm, d), lambda i: (i, 0))
    ins = [h, gkv, gq, wk, wv, wq]
    specs = [row, _full(gkv.shape), _full(gq.shape), _full(wk.shape), _full(wv.shape), _full(wq.shape)]
    outs = [jax.ShapeDtypeStruct((t, d), F32)] * 3
    ospecs = [row] * 3
    if prompt:
        assert tm == MOBA_BLOCK
        tps = seq // tm
        ins.append(wvt)
        specs.append(_full(wvt.shape))
        outs += [jax.ShapeDtypeStruct((t, d), BF16),
                 jax.ShapeDtypeStruct((t // seq, d, seq), BF16),
                 jax.ShapeDtypeStruct((t // tm, 1, d), F32)]
        ospecs += [row, pl.BlockSpec((None, d, tm), lambda i: (i // tps, 0, i % tps)),
                   pl.BlockSpec((None, 1, d), lambda i: (i, 0, 0))]
    return pl.pallas_call(
        functools.partial(_kvq_kernel, prompt=prompt), grid=(t // tm,), in_specs=specs,
        out_specs=ospecs, out_shape=outs, compiler_params=_cparams("parallel"), name="kvq")(*ins)


def _scan_kernel(r_ref, k_ref, v_ref, lw_ref, a_ref, g_ref, kk_ref, ka_ref, rk_ref, lnw_ref, lnb_ref,
                 z0_ref, o_ref, zout_ref, z_scr, *, C, NP):
    c = pl.program_id(2)

    @pl.when(c == 0)
    def _():
        z_scr[...] = z0_ref[...]

    lane = lax.broadcasted_iota(jnp.int32, (1, LANES), 1)
    m0 = lane < HEAD_DIM
    masks = (m0, jnp.logical_not(m0))
    pairs = range(NP)
    heads = [(p, h) for p in pairs for h in range(2)]

    def hsum(x):
        s0 = jnp.sum(jnp.where(m0, x, 0.0), axis=-1, keepdims=True)
        s1 = jnp.sum(jnp.where(m0, 0.0, x), axis=-1, keepdims=True)
        return jnp.where(m0, s0, s1)

    def tile(ref, p):
        return ref[:, p * LANES:(p + 1) * LANES]

    ri = lax.broadcasted_iota(jnp.int32, (C, C), 0)
    ci = lax.broadcasted_iota(jnp.int32, (C, C), 1)
    ltri = jnp.where(ci <= ri, 1.0, 0.0).astype(BF16)
    strict_c = ci < ri
    incl_c = ci <= ri
    rw = lax.broadcasted_iota(jnp.int32, (C, 2 * C), 0)
    cw = lax.broadcasted_iota(jnp.int32, (C, 2 * C), 1)
    strict_w = jnp.where(cw < C, cw, cw - C) < rw
    right_w = cw >= C
    eye_w = cw - C == rw
    zero_c = jnp.zeros((C, LANES), BF16)
    rz = lax.broadcasted_iota(jnp.int32, (LANES, LANES), 0)
    cz = lax.broadcasted_iota(jnp.int32, (LANES, LANES), 1)
    same_head = (rz < HEAD_DIM) == (cz < HEAD_DIM)

    r = [tile(r_ref, p) for p in pairs]
    k = [tile(k_ref, p) for p in pairs]
    v = [tile(v_ref, p) for p in pairs]
    lw = [tile(lw_ref, p) for p in pairs]
    a = [tile(a_ref, p) for p in pairs]

    def cumsum(x):
        l1 = x.astype(BF16)
        rem = x - l1.astype(F32)
        l2 = rem.astype(BF16)
        l3 = (rem - l2.astype(F32)).astype(BF16)
        return _dot(ltri, l1) + (_dot(ltri, l2) + _dot(ltri, l3))

    cl = [cumsum(lw[p]) for p in pairs]
    z = [z_scr[p] for p in pairs]
    zs = [_bf(z[p]) for p in pairs]

    kmod, ars, rhs_cols, bk_t, gc_col, vs = [], [], [], [], [], []
    for p in pairs:
        kkr = k[p] * tile(kk_ref, p)
        kk = kkr / jnp.maximum(jnp.sqrt(hsum(kkr * kkr)), 1e-12)
        beta = kk * a[p]
        km = k[p] * (1.0 + (a[p] - 1.0) * tile(ka_ref, p))
        kmod.append(km)
        igam = jnp.exp(-cl[p])
        a_t = -kk * jnp.exp(cl[p] - lw[p])
        r_t = r[p] * jnp.exp(cl[p])
        b_h = beta * igam
        k_h = km * igam
        gout = jnp.exp(cl[p][C - 1:C, :] - cl[p])
        ars.append(_bf(jnp.concatenate([a_t, r_t], axis=0)))
        rhs_cols.append(_bf(jnp.concatenate([b_h, b_h, k_h], axis=0)))
        bk_t.append(_bf(jnp.concatenate([(beta * gout).T, (km * gout).T], axis=1)))
        gc_col.append(jnp.exp(jnp.sum(lw[p].T, axis=1, keepdims=True)))
        vs.append(_bf(v[p]))

    xs = [_dot(ars[p], zs[p]) for p in pairs]
    gfull = [_dot(jnp.where(masks[h], ars[p], jnp.zeros_like(ars[p])), rhs_cols[p], _NT)
             for p, h in heads]
    w = [jnp.where(strict_w, g[:C, :2 * C], 0.0) for g in gfull]
    gk = [_bf(jnp.concatenate([jnp.where(strict_c, g[:C, 2 * C:], 0.0),
                               jnp.where(incl_c, g[C:, 2 * C:], 0.0)], axis=0)) for g in gfull]
    arb = [_bf(jnp.where(incl_c, g[C:, :C], 0.0)) for g in gfull]
    av = [_dot(gk[i], vs[p]) for i, (p, h) in enumerate(heads)]
    avs = [jnp.where(m0, av[2 * p], av[2 * p + 1]) for p in pairs]

    for _ in range(int(math.log2(C))):
        wb = [_bf(x) for x in w]
        w = [jnp.where(right_w, w[i], 0.0) + _dot(wb[i][:, :C], wb[i]) for i in range(len(heads))]
    minv = [_bf(jnp.where(right_w, x, 0.0) + jnp.where(eye_w, 1.0, 0.0)) for x in w]

    rhs = [jnp.concatenate([zero_c, _bf(xs[p][:C] + avs[p][:C])], axis=0) for p in pairs]
    uh = [_dot(minv[i], rhs[p]) for i, (p, h) in enumerate(heads)]
    u = [jnp.where(m0, uh[2 * p], uh[2 * p + 1]) for p in pairs]
    us = [_bf(u[p]) for p in pairs]
    oh = [_dot(arb[i], us[p]) for i, (p, h) in enumerate(heads)]
    zd = [_dot(bk_t[p], jnp.concatenate([us[p], vs[p]], axis=0)) for p in pairs]

    inv_n = 1.0 / HEAD_DIM
    for p in pairs:
        z_new = jnp.where(same_head, z[p] * gc_col[p] + zd[p], 0.0)
        z_scr[p] = z_new

        @pl.when(c == pl.num_programs(2) - 1)
        def _():
            zout_ref[p] = z_new

        o = xs[p][C:] + avs[p][C:] + jnp.where(m0, oh[2 * p], oh[2 * p + 1])
        mean = hsum(o) * inv_n
        dlt = o - mean
        var = hsum(dlt * dlt) * inv_n
        on = dlt * lax.rsqrt(var + LNX_EPS) * tile(lnw_ref, p) + tile(lnb_ref, p)
        bonus = hsum(r[p] * kmod[p] * tile(rk_ref, p)) * v[p]
        o_ref[:, p * LANES:(p + 1) * LANES] = (on + bonus) * tile(g_ref, p)


def scan_call(r, k, v, lw, a, g, k_k, k_a, r_k, lnw, lnb, z0, nb, seq, C, NP):
    t, d = r.shape
    npairs = d // LANES
    nc = seq // C
    w = NP * LANES
    row = pl.BlockSpec((C, w), lambda b, p, c: (b * nc + c, p))
    par = pl.BlockSpec((1, w), lambda b, p, c: (0, p))
    zspec = pl.BlockSpec((None, NP, LANES, LANES), lambda b, p, c: (b, p, 0, 0))
    return pl.pallas_call(
        functools.partial(_scan_kernel, C=C, NP=NP), grid=(nb, npairs // NP, nc),
        in_specs=[row] * 6 + [par] * 5 + [zspec],
        out_specs=[row, zspec],
        out_shape=[jax.ShapeDtypeStruct((t, d), F32),
                   jax.ShapeDtypeStruct((nb, npairs, LANES, LANES), F32)],
        scratch_shapes=[pltpu.VMEM((NP, LANES, LANES), F32)],
        compiler_params=_cparams("parallel", "parallel", "arbitrary"), name="rwkv_scan")(
            r, k, v, lw, a, g, k_k, k_a, r_k, lnw, lnb, z0)


def _state_to_z(s):
    b, h = s.shape[:2]
    st = jnp.swapaxes(s, 2, 3).reshape(b, h // 2, 2, HEAD_DIM, 1, HEAD_DIM)
    eye = jnp.eye(2, dtype=F32).reshape(1, 1, 2, 1, 2, 1)
    return (st * eye).reshape(b, h // 2, LANES, LANES)


def _z_to_state(z):
    b, hp = z.shape[:2]
    z6 = z.reshape(b, hp, 2, HEAD_DIM, 2, HEAD_DIM)
    st = jnp.stack([z6[:, :, 0, :, 0, :], z6[:, :, 1, :, 1, :]], axis=2)
    return jnp.swapaxes(st.reshape(b, hp * 2, HEAD_DIM, HEAD_DIM), 2, 3)


def _top3_rows(gate, rowi, nrow):
    sel = jnp.zeros(gate.shape, jnp.bool_)
    for _ in range(MOBA_TOPK):
        mx = jnp.max(gate, axis=0, keepdims=True)
        idx = jnp.min(jnp.where(gate == mx, rowi, nrow), axis=0, keepdims=True)
        pick = jnp.logical_and(rowi == idx, mx > -jnp.inf)
        sel = jnp.logical_or(sel, pick)
        gate = jnp.where(pick, -jnp.inf, gate)
    return sel


def _moba_prompt_kernel(sl_ref, q_ref, k_ref, vt_ref, km_ref, o_ref, sel_scr, e_scr, sa_scr, sb_scr, *, nb):
    p = pl.program_id(1)
    i = pl.program_id(2)
    blk = MOBA_BLOCK
    lane = lax.broadcasted_iota(jnp.int32, (1, LANES), 1)
    m0 = lane < HEAD_DIM
    masks = (m0, jnp.logical_not(m0))
    q = q_ref[...]
    km = km_ref[...]
    rowi = lax.broadcasted_iota(jnp.int32, (nb, blk), 0)
    dist0 = (lax.broadcasted_iota(jnp.int32, (blk, blk), 1)
             - lax.broadcasted_iota(jnp.int32, (blk, blk), 0)).astype(F32)
    log2e = 1.0 / math.log(2.0)
    scale2 = log2e / math.sqrt(HEAD_DIM)

    qb, slopes = [], []
    for h in range(2):
        qh = jnp.where(masks[h], q, 0.0)
        gate = jnp.where(rowi < i, _dot3(km, qh, _NT), -jnp.inf)
        sel_scr[h] = jnp.where(_top3_rows(gate, rowi, nb), 0.0, NEG)
        qb.append((qh * scale2).astype(BF16))
        slope = sl_ref[2 * p + h] * log2e
        slopes.append(slope)
        e_scr[h] = dist0 * (-slope)

    def block_off(jb):
        return pl.multiple_of(jnp.minimum(jb, nb - 1) * blk, blk)

    def scores_into(dst, jb):
        kt = k_ref[pl.ds(block_off(jb), blk), :]
        for h in range(2):
            dst[h] = _dot(kt, qb[h], _NT)

    def consume(src, jb, rowvec, carry, causal):
        vt = vt_ref[:, pl.ds(block_off(jb), blk)]
        new = []
        for h in range(2):
            m, l, acc = carry[h]
            s = src[h] + e_scr[h]
            if causal:
                s = jnp.where(dist0 >= 0, s, NEG)
            m_new = jnp.maximum(m, jnp.max(s, axis=0, keepdims=True) + rowvec[h])
            pr = jnp.exp2(s - (m_new - rowvec[h]))
            alpha = jnp.exp2(m - m_new)
            l = alpha * l + jnp.sum(pr, axis=0, keepdims=True)
            acc = alpha * acc + _dot(vt[h * HEAD_DIM:(h + 1) * HEAD_DIM, :], pr.astype(BF16))
            new.append((m_new, l, acc))
        return tuple(new)

    def past_rowvec(jb):
        gap = ((i - jb) * blk).astype(F32)
        jc = jnp.minimum(jb, nb - 1)
        return [jnp.where(jb < i, sel_scr[h, pl.ds(jc, 1), :] - slopes[h] * gap, NEG) for h in range(2)]

    scores_into(sb_scr, i)
    scores_into(sa_scr, 0)
    init = tuple((jnp.full((1, blk), NEG, F32), jnp.zeros((1, blk), F32),
                  jnp.zeros((HEAD_DIM, blk), F32)) for _ in range(2))
    zero_row = jnp.zeros((1, blk), F32)
    carry = consume(sb_scr, i, (zero_row, zero_row), init, True)

    def body(jj, carry):
        j0 = 2 * jj
        scores_into(sb_scr, j0 + 1)
        carry = consume(sa_scr, j0, past_rowvec(j0), carry, False)
        scores_into(sa_scr, j0 + 2)
        return consume(sb_scr, j0 + 1, past_rowvec(j0 + 1), carry, False)

    carry = lax.fori_loop(0, (i + 1) // 2, body, carry)
    out_t = jnp.concatenate([carry[h][2] / carry[h][1] for h in range(2)], axis=0)
    o_ref[...] = out_t.T


def moba_prompt_call(slopes, q, kb, vt, kmean, nbatch, seq):
    t, d = q.shape
    nb = seq // MOBA_BLOCK
    npairs = d // LANES
    qspec = pl.BlockSpec((MOBA_BLOCK, LANES), lambda b, p, i: (b * nb + i, p))
    tile = pltpu.VMEM((2, MOBA_BLOCK, MOBA_BLOCK), F32)
    return pl.pallas_call(
        functools.partial(_moba_prompt_kernel, nb=nb), grid=(nbatch, npairs, nb),
        in_specs=[pl.BlockSpec(memory_space=pltpu.SMEM), qspec,
                  pl.BlockSpec((seq, LANES), lambda b, p, i: (b, p)),
                  pl.BlockSpec((None, LANES, seq), lambda b, p, i: (b, p, 0)),
                  pl.BlockSpec((None, nb, LANES), lambda b, p, i: (b, 0, p))],
        out_specs=qspec, out_shape=jax.ShapeDtypeStruct((t, d), F32),
        scratch_shapes=[pltpu.VMEM((2, nb, MOBA_BLOCK), F32), tile, tile, tile],
        compiler_params=_cparams("parallel", "parallel", "arbitrary"), name="moba_prompt")(
            slopes, q, kb, vt, kmean)


def _head_block(pa_ref, pb_ref, h):
    rows = pl.ds(h, PAGE_SIZE, stride=N_HEADS)
    return jnp.concatenate([pa_ref[rows, :], pb_ref[rows, :]], axis=0)


def _sample_scores_kernel(pt_ref, q_ref, ka_ref, kb_ref, sc_ref, g_ref):
    del pt_ref
    j = pl.program_id(1)
    lane = lax.broadcasted_iota(jnp.int32, g_ref.shape[1:], 1)

    @pl.when(j == 0)
    def _():
        g_ref[...] = jnp.zeros(g_ref.shape, F32)

    kh = [_head_block(ka_ref, kb_ref, h) for h in range(N_HEADS)]
    qh = [q_ref[h] for h in range(N_HEADS)]
    for h in range(N_HEADS):
        sc_ref[h] = _dot(qh[h].astype(BF16), kh[h].astype(BF16), _NT)
    for h in range(N_HEADS):
        kmean = jnp.sum(kh[h], axis=0, keepdims=True) * (1.0 / MOBA_BLOCK)
        gcol = jnp.sum(qh[h] * kmean, axis=1, keepdims=True)
        g_ref[h] = jnp.where(lane == j, gcol, g_ref[h])


def _paged_specs(page_table, page_shape):
    npg = page_table.shape[1]
    ppb = MOBA_BLOCK // PAGE_SIZE
    assert ppb == 2
    return [pl.BlockSpec((None,) + page_shape,
                         functools.partial(lambda b, j, pt, o: (pt[b * npg + ppb * j + o], 0, 0), o=o))
            for o in range(ppb)]


def sample_scores_call(page_table, q4, cache_k):
    bs, nh, rows, dh = q4.shape
    nbp = page_table.shape[1] * PAGE_SIZE // MOBA_BLOCK
    per_b = lambda last: pl.BlockSpec((None, nh, rows, last), lambda b, j, pt: (b, 0, 0, 0))
    gs = pltpu.PrefetchScalarGridSpec(
        num_scalar_prefetch=1, grid=(bs, nbp),
        in_specs=[per_b(dh)] + _paged_specs(page_table, (PAGE_SIZE * nh, dh)),
        out_specs=[pl.BlockSpec((None, nh, rows, MOBA_BLOCK), lambda b, j, pt: (b, 0, 0, j)), per_b(nbp)])
    return pl.pallas_call(
        _sample_scores_kernel, grid_spec=gs,
        out_shape=[jax.ShapeDtypeStruct((bs, nh, rows, nbp * MOBA_BLOCK), F32),
                   jax.ShapeDtypeStruct((bs, nh, rows, nbp), F32)],
        compiler_params=_cparams("parallel", "arbitrary"), name="sample_scores")(
            page_table.reshape(-1), q4, cache_k, cache_k)


def _sample_attend_kernel(pt_ref, sl_ref, sc_ref, g_ref, q_ref, ko_ref, vo_ref, va_ref, vb_ref, o_ref,
                          m_scr, l_scr, acc_scr, sel_scr, *, nbp, past_len):
    del pt_ref
    j = pl.program_id(1)
    rows = q_ref.shape[1]
    scale = 1.0 / math.sqrt(HEAD_DIM)
    tq = lax.broadcasted_iota(jnp.int32, (rows, 1), 0)
    heads = range(N_HEADS)

    @pl.when(j == 0)
    def _():
        lane = lax.broadcasted_iota(jnp.int32, (rows, nbp), 1)
        tk = lax.broadcasted_iota(jnp.int32, (rows, rows), 1)
        for h in heads:
            gate = g_ref[h]
            sel = jnp.zeros(gate.shape, jnp.bool_)
            for _ in range(MOBA_TOPK):
                mx = jnp.max(gate, axis=1, keepdims=True)
                idx = jnp.min(jnp.where(gate == mx, lane, nbp), axis=1, keepdims=True)
                pick = jnp.logical_and(lane == idx, mx > -jnp.inf)
                sel = jnp.logical_or(sel, pick)
                gate = jnp.where(pick, -jnp.inf, gate)
            sel_scr[h] = jnp.where(sel, 0.0, NEG)
            s = _dot(q_ref[h].astype(BF16), ko_ref[h].astype(BF16), _NT) * scale
            s = jnp.where(tk <= tq, s - sl_ref[h] * (tq - tk).astype(F32), NEG)
            m = jnp.max(s, axis=1, keepdims=True)
            pr = jnp.exp(s - m)
            m_scr[h] = m
            l_scr[h] = jnp.sum(pr, axis=1, keepdims=True)
            acc_scr[h] = _dot(pr.astype(BF16), vo_ref[h].astype(BF16))

    lane_k = lax.broadcasted_iota(jnp.int32, (rows, MOBA_BLOCK), 1)
    dist = ((past_len + tq) - (j * MOBA_BLOCK + lane_k)).astype(F32)
    lane_b = lax.broadcasted_iota(jnp.int32, (rows, nbp), 1)
    vh = [_head_block(va_ref, vb_ref, h).astype(BF16) for h in heads]
    prs, alphas = [], []
    for h in heads:
        bias = jnp.sum(jnp.where(lane_b == j, sel_scr[h], 0.0), axis=1, keepdims=True)
        s = sc_ref[h] * scale - sl_ref[h] * dist + bias
        m_old = m_scr[h]
        m_new = jnp.maximum(m_old, jnp.max(s, axis=1, keepdims=True))
        alpha = jnp.exp(m_old - m_new)
        pr = jnp.exp(s - m_new)
        m_scr[h] = m_new
        l_scr[h] = alpha * l_scr[h] + jnp.sum(pr, axis=1, keepdims=True)
        prs.append(pr.astype(BF16))
        alphas.append(alpha)
    for h in heads:
        acc_scr[h] = alphas[h] * acc_scr[h] + _dot(prs[h], vh[h])

    @pl.when(j == nbp - 1)
    def _():
        for h in heads:
            o_ref[h] = acc_scr[h] / l_scr[h]


def sample_attend_call(page_table, slopes, scores, gate, q4, k_own, v_own, cache_v):
    bs, nh, rows, dh = q4.shape
    npg = page_table.shape[1]
    nbp = npg * PAGE_SIZE // MOBA_BLOCK
    per_b = lambda last: pl.BlockSpec((None, nh, rows, last), lambda b, j, pt: (b, 0, 0, 0))
    gs = pltpu.PrefetchScalarGridSpec(
        num_scalar_prefetch=1, grid=(bs, nbp),
        in_specs=[pl.BlockSpec(memory_space=pltpu.SMEM),
                  pl.BlockSpec((None, nh, rows, MOBA_BLOCK), lambda b, j, pt: (b, 0, 0, j)),
                  per_b(nbp), per_b(dh), per_b(dh), per_b(dh)] + _paged_specs(page_table, (PAGE_SIZE * nh, dh)),
        out_specs=per_b(dh),
        scratch_shapes=[pltpu.VMEM((nh, rows, 1), F32), pltpu.VMEM((nh, rows, 1), F32),
                        pltpu.VMEM((nh, rows, dh), F32), pltpu.VMEM((nh, rows, nbp), F32)])
    return pl.pallas_call(
        functools.partial(_sample_attend_kernel, nbp=nbp, past_len=npg * PAGE_SIZE),
        grid_spec=gs, out_shape=jax.ShapeDtypeStruct((bs, nh, rows, dh), F32),
        compiler_params=_cparams("parallel", "arbitrary"), name="sample_attend")(
            page_table.reshape(-1), slopes, scores, gate, q4, k_own, v_own, cache_v, cache_v)


def _row_tile(t):
    for tm in (512, 256, 128):
        if t % tm == 0:
            return tm
    raise ValueError(f"token count {t} is not a multiple of 128")


def _ffn_tile(f):
    for nf in (1, 2, 4, 11, 22):
        if f % nf == 0 and (f // nf) % LANES == 0 and f // nf <= 1536:
            return f // nf
    raise ValueError(f"unsupported FFN width {f}")


def _trunk(x, p, wkv_in, shift_in, W, cache=None):
    b, t, d = x.shape
    nt = b * t
    tm = _row_tile(nt)
    h = x.reshape(nt, d)
    row = lambda v: v.reshape(1, -1)

    hn = rmsnorm_call(h, row(W['norm_mix'][0]), tm)
    hn3 = hn.reshape(b, t, d)
    xp = jnp.concatenate([shift_in[:, None, :], hn3[:, :-1]], axis=1).reshape(nt, d)
    shift_out = hn3[:, -1]
    r, k, v, lw, a, g = rwkv_proj_call(
        hn, xp, W['rwkv_mu'][0], W['w_r'], W['w_k'], W['w_v'], W['rwkv_w1'], W['rwkv_w2'],
        W['rwkv_a1'], W['rwkv_a2'], W['rwkv_g1'], W['rwkv_g2'],
        row(W['rwkv_w0'][0]), row(W['rwkv_a0'][0]), tm)
    chunk = SCAN_CHUNK if t % SCAN_CHUNK == 0 else 8
    tp = -(-t // chunk) * chunk
    seqs = (r, k, v, lw, a, g)
    if tp != t:
        seqs = tuple(jnp.pad(s.reshape(b, t, d), ((0, 0), (0, tp - t), (0, 0))).reshape(b * tp, d)
                     for s in seqs)
    o, zf = scan_call(*seqs, row(W['rwkv_k_k'][0]), row(W['rwkv_k_a'][0]), row(W['rwkv_r_k'][0]),
                      row(W['rwkv_lnx_w'][0]), row(W['rwkv_lnx_b'][0]), _state_to_z(wkv_in), b, tp, chunk,
                      d // LANES)
    if tp != t:
        o = o.reshape(b, tp, d)[:, :t].reshape(nt, d)
    wkv_out = _z_to_state(zf)
    h = matmul_res_call(o, W['rwkv_w_o'], h, tm)
    tf = _ffn_tile(W['ffn_w_gate'][0].shape[1])
    h = ffn_call(h, row(W['norm_ffn'][0]), W['ffn_w_gate'][0], W['ffn_w_up'][0], W['ffn_w_down'][0], tm, tf)
    h = ple_call(h, p[0].reshape(nt, -1), row(W['norm_ple'][0]), W['ple_w_gate'][0], W['ple_w_proj'][0], tm)

    slopes = jnp.exp2(-8.0 * jnp.arange(1, N_HEADS + 1, dtype=F32) / N_HEADS)
    if cache is None:
        k_new, v_new, q, kb, vt, kmean = kvq_call(
            h, row(W['norm_kv']), row(W['norm_mix'][1]), W['kv_w_k'], W['kv_w_v'], W['attn_w_q'],
            MOBA_BLOCK, wvt=W['kv_w_vT'], seq=t)
        attn = moba_prompt_call(slopes, q, kb, vt, kmean.reshape(b, t // MOBA_BLOCK, d), b, t)
    else:
        cache_k, cache_v, page_table = cache
        k_new, v_new, q = kvq_call(h, row(W['norm_kv']), row(W['norm_mix'][1]),
                                   W['kv_w_k'], W['kv_w_v'], W['attn_w_q'], tm)

        def per_head(z):
            z4 = jnp.swapaxes(z.reshape(b, t, N_HEADS, HEAD_DIM), 1, 2)
            return jnp.pad(z4, ((0, 0), (0, 0), (0, 8 - t), (0, 0)))

        q4 = per_head(q)
        pages = lambda c: c.reshape(c.shape[0], PAGE_SIZE * N_HEADS, HEAD_DIM)
        scores, gate = sample_scores_call(page_table, q4, pages(cache_k))
        attn4 = sample_attend_call(page_table, slopes, scores, gate, q4, per_head(k_new), per_head(v_new),
                                   pages(cache_v))
        attn = jnp.swapaxes(attn4[:, :, :t], 1, 2).reshape(nt, d)
    h = matmul_res_call(attn, W['attn_w_o'], h, tm)
    h = ffn_call(h, row(W['norm_ffn'][1]), W['ffn_w_gate'][1], W['ffn_w_up'][1], W['ffn_w_down'][1], tm, tf)
    y = ple_call(h, p[1].reshape(nt, -1), row(W['norm_ple'][1]), W['ple_w_gate'][1], W['ple_w_proj'][1], tm,
                 g_final=row(W['norm_final']))
    shp = (b, t, N_HEADS, HEAD_DIM)
    return (y.reshape(b, t, d), wkv_out[None], shift_out[None], k_new.reshape(shp), v_new.reshape(shp))


def kernel(x_prompt, x_sample, p_prompt, p_sample, state_wkv, state_shift, cache_k, cache_v, page_table,
           norm_mix, norm_ffn, norm_ple, norm_kv, norm_final, rwkv_mu, rwkv_w_rkv, rwkv_w_o, rwkv_w0,
           rwkv_w1, rwkv_w2, rwkv_a0, rwkv_a1, rwkv_a2, rwkv_g1, rwkv_g2, rwkv_k_k, rwkv_k_a, rwkv_r_k,
           rwkv_lnx_w, rwkv_lnx_b, attn_w_q, attn_w_o, kv_w_k, kv_w_v, ffn_w_gate, ffn_w_up, ffn_w_down,
           ple_w_proj, ple_w_gate):
    assert norm_mix.shape[0] == 2 and state_wkv.shape[0] == 1, "one RWKV layer then one MoBA layer"
    bf = lambda w: w.astype(BF16)
    W = dict(norm_mix=norm_mix, norm_ffn=norm_ffn, norm_ple=norm_ple, norm_kv=norm_kv, norm_final=norm_final,
             rwkv_mu=rwkv_mu, w_r=bf(rwkv_w_rkv[0, 0]), w_k=bf(rwkv_w_rkv[0, 1]), w_v=bf(rwkv_w_rkv[0, 2]),
             rwkv_w_o=bf(rwkv_w_o[0]), rwkv_w0=rwkv_w0, rwkv_w1=bf(rwkv_w1[0]), rwkv_w2=bf(rwkv_w2[0]),
             rwkv_a0=rwkv_a0, rwkv_a1=bf(rwkv_a1[0]), rwkv_a2=bf(rwkv_a2[0]), rwkv_g1=bf(rwkv_g1[0]),
             rwkv_g2=bf(rwkv_g2[0]), rwkv_k_k=rwkv_k_k, rwkv_k_a=rwkv_k_a, rwkv_r_k=rwkv_r_k,
             rwkv_lnx_w=rwkv_lnx_w, rwkv_lnx_b=rwkv_lnx_b, attn_w_q=bf(attn_w_q[0]), attn_w_o=bf(attn_w_o[0]),
             kv_w_k=bf(kv_w_k), kv_w_v=bf(kv_w_v), kv_w_vT=bf(kv_w_v.T),
             ffn_w_gate=bf(ffn_w_gate), ffn_w_up=bf(ffn_w_up), ffn_w_down=bf(ffn_w_down),
             ple_w_proj=bf(ple_w_proj), ple_w_gate=bf(ple_w_gate))
    bp = x_prompt.shape[0]
    d = x_prompt.shape[2]
    wkv0 = jnp.zeros((bp, N_HEADS, HEAD_DIM, HEAD_DIM), F32)
    shift0 = jnp.zeros((bp, d), F32)
    y_p, wkv_p, shift_p, k_p, v_p = _trunk(x_prompt, p_prompt, wkv0, shift0, W)
    y_s, wkv_s, shift_s, k_s, v_s = _trunk(x_sample, p_sample, state_wkv[0], state_shift[0], W,
                                           cache=(cache_k, cache_v, page_table))
    return (y_p, y_s, wkv_p, shift_p, k_p, v_p, wkv_s, shift_s, k_s, v_s)
```

```python
import functools
import math

import jax
import jax.numpy as jnp
from jax import lax
from jax.experimental import pallas as pl
from jax.experimental.pallas import tpu as pltpu

F32, BF16 = jnp.float32, jnp.bfloat16
N_HEADS = 16
HEAD_DIM = 64
LANES = 128
MOBA_BLOCK = 256
MOBA_TOPK = 3
PAGE_SIZE = 128
RMS_EPS = 1e-6
LNX_EPS = 64e-5
NEG = -1e30
VMEM_LIMIT_BYTES = 56 * 1024 * 1024
SCAN_CHUNK = 64
POS_LANES = 3
VT_ROWS = 80


def _cparams(*sem):
    return pltpu.CompilerParams(dimension_semantics=sem, vmem_limit_bytes=VMEM_LIMIT_BYTES)


def _rms(x, g):
    return x * lax.rsqrt(jnp.mean(x * x, axis=-1, keepdims=True) + RMS_EPS) * g


def _sigmoid(x):
    return 1.0 / (1.0 + jnp.exp(-x))


_NN = (((1,), (0,)), ((), ()))
_NT = (((1,), (1,)), ((), ()))


def _dot(a, b, dims=_NN):
    return lax.dot_general(a, b, dims, preferred_element_type=F32)


def _bf(x):
    return x.astype(BF16)


def _bdot(a, w):
    return _dot(a.astype(BF16), w)


def _split2(x):
    hi = x.astype(BF16)
    lo = (x - hi.astype(F32)).astype(BF16)
    return hi, lo


def _dot3(a, b, dims=_NN):
    ah, al = _split2(a)
    bh, bl = _split2(b)
    return _dot(ah, bh, dims) + (_dot(ah, bl, dims) + _dot(al, bh, dims))


def _full(shape):
    n = len(shape)
    return pl.BlockSpec(shape, lambda *_: (0,) * n)


def _rms_kernel(x_ref, g_ref, o_ref):
    o_ref[...] = _rms(x_ref[...], g_ref[...])


def rmsnorm_call(x, g, tm):
    t, d = x.shape
    row = pl.BlockSpec((tm, d), lambda i: (i, 0))
    return pl.pallas_call(
        _rms_kernel, grid=(t // tm,), in_specs=[row, _full((1, d))], out_specs=row,
        out_shape=jax.ShapeDtypeStruct((t, d), F32), compiler_params=_cparams("parallel"),
        name="rmsnorm")(x, g)


def _rwkv_proj_kernel(hn_ref, xp_ref, mu_ref, wr, wk, wv, w1, w2, a1, a2, g1, g2, w0, a0,
                      r_o, k_o, v_o, lw_o, a_o, g_o):
    hn = hn_ref[...]
    xx = xp_ref[...] - hn

    def mix(i):
        return hn + xx * mu_ref[i:i + 1, :]

    r_o[...] = _bdot(mix(0), wr[...])
    k_o[...] = _bdot(mix(2), wk[...])
    v_o[...] = _bdot(mix(3), wv[...])
    u = w0[...] + _bdot(jnp.tanh(_bdot(mix(1), w1[...])), w2[...])
    lw_o[...] = -math.exp(-0.5) * _sigmoid(u)
    a_o[...] = _sigmoid(a0[...] + _bdot(_bdot(mix(4), a1[...]), a2[...]))
    g_o[...] = _bdot(_sigmoid(_bdot(mix(5), g1[...])), g2[...])


def rwkv_proj_call(hn, xp, mu, wr, wk, wv, w1, w2, a1, a2, g1, g2, w0, a0, tm):
    t, d = hn.shape
    row = pl.BlockSpec((tm, d), lambda i: (i, 0))
    ws = [mu, wr, wk, wv, w1, w2, a1, a2, g1, g2, w0, a0]
    return pl.pallas_call(
        _rwkv_proj_kernel, grid=(t // tm,),
        in_specs=[row, row] + [_full(w.shape) for w in ws],
        out_specs=[row] * 6, out_shape=[jax.ShapeDtypeStruct((t, d), F32)] * 6,
        compiler_params=_cparams("parallel"), name="rwkv_proj")(hn, xp, *ws)


def _matmul_res_kernel(x_ref, w_ref, res_ref, o_ref):
    o_ref[...] = res_ref[...] + _bdot(x_ref[...], w_ref[...])


def matmul_res_call(x, w, res, tm):
    t, d = x.shape
    row = pl.BlockSpec((tm, d), lambda i: (i, 0))
    return pl.pallas_call(
        _matmul_res_kernel, grid=(t // tm,), in_specs=[row, _full(w.shape), row], out_specs=row,
        out_shape=jax.ShapeDtypeStruct((t, d), F32), compiler_params=_cparams("parallel"),
        name="matmul_res")(x, w, res)


def _ffn_kernel(h_ref, g_ref, wg_ref, wu_ref, wd_ref, o_ref, xn_scr):
    @pl.when(pl.program_id(1) == 0)
    def _():
        h = h_ref[...]
        xn_scr[...] = _rms(h, g_ref[...]).astype(BF16)
        o_ref[...] = h

    xn = xn_scr[...]
    gt = _dot(xn, wg_ref[...])
    up = _dot(xn, wu_ref[...])
    act = gt * _sigmoid(gt) * up
    o_ref[...] += _bdot(act, wd_ref[...])


def ffn_call(h, g, wg, wu, wd, tm, tf):
    t, d = h.shape
    f = wg.shape[1]
    row = pl.BlockSpec((tm, d), lambda i, j: (i, 0))
    return pl.pallas_call(
        _ffn_kernel, grid=(t // tm, f // tf),
        in_specs=[row, pl.BlockSpec((1, d), lambda i, j: (0, 0)),
                  pl.BlockSpec((d, tf), lambda i, j: (0, j)),
                  pl.BlockSpec((d, tf), lambda i, j: (0, j)),
                  pl.BlockSpec((tf, d), lambda i, j: (j, 0))],
        out_specs=row, out_shape=jax.ShapeDtypeStruct((t, d), F32),
        scratch_shapes=[pltpu.VMEM((tm, d), BF16)],
        compiler_params=_cparams("parallel", "arbitrary"), name="ffn")(h, g, wg, wu, wd)


def _ple_kernel(h_ref, p_ref, g_ref, wg_ref, wp_ref, *rest, final):
    h = h_ref[...]
    gate = _sigmoid(_bdot(_rms(h, g_ref[...]), wg_ref[...]))
    hn = h + _bdot(p_ref[...], wp_ref[...]) * gate
    if final:
        gf_ref, y_ref = rest
        y_ref[...] = _rms(hn, gf_ref[...])
    else:
        rest[0][...] = hn


def ple_call(h, p, g, wg, wp, tm, g_final=None):
    t, d = h.shape
    row = pl.BlockSpec((tm, d), lambda i: (i, 0))
    prow = pl.BlockSpec((tm, p.shape[1]), lambda i: (i, 0))
    ins = [h, p, g, wg, wp]
    specs = [row, prow, _full(g.shape), _full(wg.shape), _full(wp.shape)]
    if g_final is not None:
        ins.append(g_final)
        specs.append(_full(g_final.shape))
    return pl.pallas_call(
        functools.partial(_ple_kernel, final=g_final is not None), grid=(t // tm,),
        in_specs=specs, out_specs=row, out_shape=jax.ShapeDtypeStruct((t, d), F32),
        compiler_params=_cparams("parallel"), name="ple")(*ins)


def _kvq_kernel(h_ref, gkv_ref, gq_ref, wk_ref, wv_ref, wq_ref, *rest, prompt):
    h = h_ref[...]
    hk = _rms(h, gkv_ref[...]).astype(BF16)
    k = _dot(hk, wk_ref[...])
    if prompt:
        wvt_ref, k_o, v_o, q_o, kb_o, vt_o, km_o = rest
    else:
        k_o, v_o, q_o = rest
    k_o[...] = k
    v_o[...] = _dot(hk, wv_ref[...])
    q_o[...] = _bdot(_rms(h, gq_ref[...]), wq_ref[...])
    if prompt:
        tm, d = k.shape
        lane = lax.broadcasted_iota(jnp.int32, (tm, LANES), 1)
        rpos = lax.broadcasted_iota(jnp.int32, (tm, LANES), 0).astype(F32)
        for p in range(d // LANES):
            kp = k[:, p * LANES:(p + 1) * LANES]
            for hh in range(2):
                own = (lane < HEAD_DIM) == (hh == 0)
                pos = lane - (HEAD_DIM if hh == 0 else 0)
                extra = jnp.where(jnp.logical_and(pos >= 0, pos < POS_LANES), rpos, 0.0)
                c0 = (2 * p + hh) * LANES
                kb_o[:, c0:c0 + LANES] = jnp.where(own, kp, extra).astype(BF16)
        vt = _dot(wvt_ref[...], hk, _NT)
        ones_rows = jnp.where(lax.broadcasted_iota(jnp.int32, (VT_ROWS - HEAD_DIM, tm), 0) == 0, 1.0, 0.0)
        for hd in range(d // HEAD_DIM):
            vt_o[hd * VT_ROWS:hd * VT_ROWS + HEAD_DIM, :] = vt[hd * HEAD_DIM:(hd + 1) * HEAD_DIM, :].astype(BF16)
            vt_o[hd * VT_ROWS + HEAD_DIM:(hd + 1) * VT_ROWS, :] = ones_rows.astype(BF16)
        km_o[...] = jnp.sum(k, axis=0, keepdims=True) * (1.0 / MOBA_BLOCK)


def kvq_call(h, gkv, gq, wk, wv, wq, tm, wvt=None, seq=None):
    t, d = h.shape
    prompt = wvt is not None
    row = pl.BlockSpec((tm, d), lambda i: (i, 0))
    ins = [h, gkv, gq, wk, wv, wq]
    specs = [row, _full(gkv.shape), _full(gq.shape), _full(wk.shape), _full(wv.shape), _full(wq.shape)]
    outs = [jax.ShapeDtypeStruct((t, d), F32)] * 3
    ospecs = [row] * 3
    if prompt:
        assert tm == MOBA_BLOCK
        tps = seq // tm
        nh = d // HEAD_DIM
        ins.append(wvt)
        specs.append(_full(wvt.shape))
        outs += [jax.ShapeDtypeStruct((t, nh * LANES), BF16),
                 jax.ShapeDtypeStruct((t // seq, nh * VT_ROWS, seq), BF16),
                 jax.ShapeDtypeStruct((t // tm, 1, d), F32)]
        ospecs += [pl.BlockSpec((tm, nh * LANES), lambda i: (i, 0)),
                   pl.BlockSpec((None, nh * VT_ROWS, tm), lambda i: (i // tps, 0, i % tps)),
                   pl.BlockSpec((None, 1, d), lambda i: (i, 0, 0))]
    return pl.pallas_call(
        functools.partial(_kvq_kernel, prompt=prompt), grid=(t // tm,), in_specs=specs,
        out_specs=ospecs, out_shape=outs, compiler_params=_cparams("parallel"), name="kvq")(*ins)


def _scan_kernel(r_ref, k_ref, v_ref, lw_ref, a_ref, g_ref, kk_ref, ka_ref, rk_ref, lnw_ref, lnb_ref,
                 z0_ref, o_ref, zout_ref, z_scr, *, C, NP):
    c = pl.program_id(2)

    @pl.when(c == 0)
    def _():
        z_scr[...] = z0_ref[...]

    lane = lax.broadcasted_iota(jnp.int32, (1, LANES), 1)
    m0 = lane < HEAD_DIM
    masks = (m0, jnp.logical_not(m0))
    pairs = range(NP)
    heads = [(p, h) for p in pairs for h in range(2)]

    def hsum(x):
        s0 = jnp.sum(jnp.where(m0, x, 0.0), axis=-1, keepdims=True)
        s1 = jnp.sum(jnp.where(m0, 0.0, x), axis=-1, keepdims=True)
        return jnp.where(m0, s0, s1)

    def tile(ref, p):
        return ref[:, p * LANES:(p + 1) * LANES]

    ri = lax.broadcasted_iota(jnp.int32, (C, C), 0)
    ci = lax.broadcasted_iota(jnp.int32, (C, C), 1)
    ltri = jnp.where(ci <= ri, 1.0, 0.0).astype(BF16)
    strict_c = ci < ri
    incl_c = ci <= ri
    rw = lax.broadcasted_iota(jnp.int32, (C, 2 * C), 0)
    cw = lax.broadcasted_iota(jnp.int32, (C, 2 * C), 1)
    strict_w = jnp.where(cw < C, cw, cw - C) < rw
    right_w = cw >= C
    eye_w = cw - C == rw
    zero_c = jnp.zeros((C, LANES), BF16)
    rz = lax.broadcasted_iota(jnp.int32, (LANES, LANES), 0)
    cz = lax.broadcasted_iota(jnp.int32, (LANES, LANES), 1)
    same_head = (rz < HEAD_DIM) == (cz < HEAD_DIM)

    r = [tile(r_ref, p) for p in pairs]
    k = [tile(k_ref, p) for p in pairs]
    v = [tile(v_ref, p) for p in pairs]
    lw = [tile(lw_ref, p) for p in pairs]
    a = [tile(a_ref, p) for p in pairs]

    def cumsum(x):
        l1 = x.astype(BF16)
        rem = x - l1.astype(F32)
        l2 = rem.astype(BF16)
        l3 = (rem - l2.astype(F32)).astype(BF16)
        return _dot(ltri, l1) + (_dot(ltri, l2) + _dot(ltri, l3))

    cl = [cumsum(lw[p]) for p in pairs]
    z = [z_scr[p] for p in pairs]
    zs = [_bf(z[p]) for p in pairs]

    kmod, ars, rhs_cols, bk_t, gc_col, vs = [], [], [], [], [], []
    for p in pairs:
        kkr = k[p] * tile(kk_ref, p)
        kk = kkr / jnp.maximum(jnp.sqrt(hsum(kkr * kkr)), 1e-12)
        beta = kk * a[p]
        km = k[p] * (1.0 + (a[p] - 1.0) * tile(ka_ref, p))
        kmod.append(km)
        igam = jnp.exp(-cl[p])
        a_t = -kk * jnp.exp(cl[p] - lw[p])
        r_t = r[p] * jnp.exp(cl[p])
        b_h = beta * igam
        k_h = km * igam
        gout = jnp.exp(cl[p][C - 1:C, :] - cl[p])
        ars.append(_bf(jnp.concatenate([a_t, r_t], axis=0)))
        rhs_cols.append(_bf(jnp.concatenate([b_h, b_h, k_h], axis=0)))
        bk_t.append(_bf(jnp.concatenate([(beta * gout).T, (km * gout).T], axis=1)))
        gc_col.append(jnp.exp(jnp.sum(lw[p].T, axis=1, keepdims=True)))
        vs.append(_bf(v[p]))

    xs = [_dot(ars[p], zs[p]) for p in pairs]
    gfull = [_dot(jnp.where(masks[h], ars[p], jnp.zeros_like(ars[p])), rhs_cols[p], _NT)
             for p, h in heads]
    w = [jnp.where(strict_w, g[:C, :2 * C], 0.0) for g in gfull]
    gk = [_bf(jnp.concatenate([jnp.where(strict_c, g[:C, 2 * C:], 0.0),
                               jnp.where(incl_c, g[C:, 2 * C:], 0.0)], axis=0)) for g in gfull]
    arb = [_bf(jnp.where(incl_c, g[C:, :C], 0.0)) for g in gfull]
    av = [_dot(gk[i], vs[p]) for i, (p, h) in enumerate(heads)]
    avs = [jnp.where(m0, av[2 * p], av[2 * p + 1]) for p in pairs]

    for _ in range(int(math.log2(C))):
        wb = [_bf(x) for x in w]
        w = [jnp.where(right_w, w[i], 0.0) + _dot(wb[i][:, :C], wb[i]) for i in range(len(heads))]
    minv = [_bf(jnp.where(right_w, x, 0.0) + jnp.where(eye_w, 1.0, 0.0)) for x in w]

    rhs = [jnp.concatenate([zero_c, _bf(xs[p][:C] + avs[p][:C])], axis=0) for p in pairs]
    uh = [_dot(minv[i], rhs[p]) for i, (p, h) in enumerate(heads)]
    u = [jnp.where(m0, uh[2 * p], uh[2 * p + 1]) for p in pairs]
    us = [_bf(u[p]) for p in pairs]
    oh = [_dot(arb[i], us[p]) for i, (p, h) in enumerate(heads)]
    zd = [_dot(bk_t[p], jnp.concatenate([us[p], vs[p]], axis=0)) for p in pairs]

    inv_n = 1.0 / HEAD_DIM
    for p in pairs:
        z_new = jnp.where(same_head, z[p] * gc_col[p] + zd[p], 0.0)
        z_scr[p] = z_new

        @pl.when(c == pl.num_programs(2) - 1)
        def _():
            zout_ref[p] = z_new

        o = xs[p][C:] + avs[p][C:] + jnp.where(m0, oh[2 * p], oh[2 * p + 1])
        mean = hsum(o) * inv_n
        dlt = o - mean
        var = hsum(dlt * dlt) * inv_n
        on = dlt * lax.rsqrt(var + LNX_EPS) * tile(lnw_ref, p) + tile(lnb_ref, p)
        bonus = hsum(r[p] * kmod[p] * tile(rk_ref, p)) * v[p]
        o_ref[:, p * LANES:(p + 1) * LANES] = (on + bonus) * tile(g_ref, p)


def scan_call(r, k, v, lw, a, g, k_k, k_a, r_k, lnw, lnb, z0, nb, seq, C, NP):
    t, d = r.shape
    npairs = d // LANES
    nc = seq // C
    w = NP * LANES
    row = pl.BlockSpec((C, w), lambda b, p, c: (b * nc + c, p))
    par = pl.BlockSpec((1, w), lambda b, p, c: (0, p))
    zspec = pl.BlockSpec((None, NP, LANES, LANES), lambda b, p, c: (b, p, 0, 0))
    return pl.pallas_call(
        functools.partial(_scan_kernel, C=C, NP=NP), grid=(nb, npairs // NP, nc),
        in_specs=[row] * 6 + [par] * 5 + [zspec],
        out_specs=[row, zspec],
        out_shape=[jax.ShapeDtypeStruct((t, d), F32),
                   jax.ShapeDtypeStruct((nb, npairs, LANES, LANES), F32)],
        scratch_shapes=[pltpu.VMEM((NP, LANES, LANES), F32)],
        compiler_params=_cparams("parallel", "parallel", "arbitrary"), name="rwkv_scan")(
            r, k, v, lw, a, g, k_k, k_a, r_k, lnw, lnb, z0)


def _state_to_z(s):
    b, h = s.shape[:2]
    st = jnp.swapaxes(s, 2, 3).reshape(b, h // 2, 2, HEAD_DIM, 1, HEAD_DIM)
    eye = jnp.eye(2, dtype=F32).reshape(1, 1, 2, 1, 2, 1)
    return (st * eye).reshape(b, h // 2, LANES, LANES)


def _z_to_state(z):
    b, hp = z.shape[:2]
    z6 = z.reshape(b, hp, 2, HEAD_DIM, 2, HEAD_DIM)
    st = jnp.stack([z6[:, :, 0, :, 0, :], z6[:, :, 1, :, 1, :]], axis=2)
    return jnp.swapaxes(st.reshape(b, hp * 2, HEAD_DIM, HEAD_DIM), 2, 3)


def _top3_rows(gate, rowi, nrow):
    sel = jnp.zeros(gate.shape, jnp.bool_)
    for _ in range(MOBA_TOPK):
        mx = jnp.max(gate, axis=0, keepdims=True)
        idx = jnp.min(jnp.where(gate == mx, rowi, nrow), axis=0, keepdims=True)
        pick = jnp.logical_and(rowi == idx, mx > -jnp.inf)
        sel = jnp.logical_or(sel, pick)
        gate = jnp.where(pick, -jnp.inf, gate)
    return sel


def _moba_prompt_kernel(sl_ref, q_ref, k_ref, vt_ref, km_ref, o_ref, sel_scr, sa_scr, sb_scr, *, nb):
    p = pl.program_id(1)
    i = pl.program_id(2)
    blk = MOBA_BLOCK
    lane = lax.broadcasted_iota(jnp.int32, (1, LANES), 1)
    m0 = lane < HEAD_DIM
    masks = (m0, jnp.logical_not(m0))
    q = q_ref[...]
    km = km_ref[...]
    rowi = lax.broadcasted_iota(jnp.int32, (nb, blk), 0)
    causal = (lax.broadcasted_iota(jnp.int32, (blk, blk), 1)
              >= lax.broadcasted_iota(jnp.int32, (blk, blk), 0))
    log2e = 1.0 / math.log(2.0)
    scale2 = log2e / math.sqrt(HEAD_DIM)

    qb, slopes = [], []
    for h in range(2):
        qh = jnp.where(masks[h], q, 0.0)
        gate = jnp.where(rowi < i, _dot3(km, qh, _NT), -jnp.inf)
        sel_scr[h] = jnp.where(_top3_rows(gate, rowi, nb), 0.0, NEG)
        slope = sl_ref[2 * p + h] * log2e
        slopes.append(slope)
        pos = lane - (HEAD_DIM if h == 0 else 0)
        rest = jnp.full((1, LANES), slope, F32)
        terms = jnp.zeros((1, LANES), F32)
        for t in range(POS_LANES):
            part = rest.astype(BF16).astype(F32)
            terms = jnp.where(pos == t, part, terms)
            rest = rest - part
        qb.append(jnp.where(masks[h], q * scale2, terms).astype(BF16))

    def block_off(jb):
        return pl.multiple_of(jnp.minimum(jb, nb - 1) * blk, blk)

    def scores_into(dst, jb):
        kt = k_ref[pl.ds(block_off(jb), blk), :]
        for h in range(2):
            dst[h] = _dot(kt[:, h * LANES:(h + 1) * LANES], qb[h], _NT)

    def consume(src, jb, rowvec, carry, own):
        vt = vt_ref[:, pl.ds(block_off(jb), blk)]
        new = []
        for h in range(2):
            m, acc = carry[h]
            s = src[h]
            if own:
                s = jnp.where(causal, s, NEG)
            m_new = jnp.maximum(m, jnp.max(s, axis=0, keepdims=True) + rowvec[h])
            pr = jnp.exp2(s - (m_new - rowvec[h]))
            acc = jnp.exp2(m - m_new) * acc + _dot(vt[h * VT_ROWS:(h + 1) * VT_ROWS, :], pr.astype(BF16))
            new.append((m_new, acc))
        return tuple(new)

    def past_rowvec(jb):
        gap = ((i - jb) * blk).astype(F32)
        jc = jnp.minimum(jb, nb - 1)
        return [jnp.where(jb < i, sel_scr[h, pl.ds(jc, 1), :] - slopes[h] * gap, NEG) for h in range(2)]

    scores_into(sb_scr, i)
    scores_into(sa_scr, 0)
    init = tuple((jnp.full((1, blk), NEG, F32), jnp.zeros((VT_ROWS, blk), F32)) for _ in range(2))
    zero_row = jnp.zeros((1, blk), F32)
    carry = consume(sb_scr, i, (zero_row, zero_row), init, True)

    def body(jj, carry):
        j0 = 2 * jj
        scores_into(sb_scr, j0 + 1)
        carry = consume(sa_scr, j0, past_rowvec(j0), carry, False)
        scores_into(sa_scr, j0 + 2)
        return consume(sb_scr, j0 + 1, past_rowvec(j0 + 1), carry, False)

    carry = lax.fori_loop(0, (i + 1) // 2, body, carry)
    out_t = jnp.concatenate([carry[h][1][:HEAD_DIM] / carry[h][1][HEAD_DIM:HEAD_DIM + 1]
                             for h in range(2)], axis=0)
    o_ref[...] = out_t.T


def moba_prompt_call(slopes, q, kb, vt, kmean, nbatch, seq):
    t, d = q.shape
    nb = seq // MOBA_BLOCK
    npairs = d // LANES
    qspec = pl.BlockSpec((MOBA_BLOCK, LANES), lambda b, p, i: (b * nb + i, p))
    tile = pltpu.VMEM((2, MOBA_BLOCK, MOBA_BLOCK), F32)
    return pl.pallas_call(
        functools.partial(_moba_prompt_kernel, nb=nb), grid=(nbatch, npairs, nb),
        in_specs=[pl.BlockSpec(memory_space=pltpu.SMEM), qspec,
                  pl.BlockSpec((seq, 2 * LANES), lambda b, p, i: (b, p)),
                  pl.BlockSpec((None, 2 * VT_ROWS, seq), lambda b, p, i: (b, p, 0)),
                  pl.BlockSpec((None, nb, LANES), lambda b, p, i: (b, 0, p))],
        out_specs=qspec, out_shape=jax.ShapeDtypeStruct((t, d), F32),
        scratch_shapes=[pltpu.VMEM((2, nb, MOBA_BLOCK), F32), tile, tile],
        compiler_params=_cparams("parallel", "parallel", "arbitrary"), name="moba_prompt")(
            slopes, q, kb, vt, kmean)


def _head_blocks(pa_ref, pb_ref):
    pa = jnp.swapaxes(pa_ref[...], 0, 1)
    pb = jnp.swapaxes(pb_ref[...], 0, 1)
    return [jnp.concatenate([pa[h], pb[h]], axis=0) for h in range(N_HEADS)]


def _sample_scores_kernel(pt_ref, q_ref, *refs, bps):
    del pt_ref
    page_refs, (sc_ref, g_ref) = refs[:2 * bps], refs[2 * bps:]
    j = pl.program_id(1)
    lane = lax.broadcasted_iota(jnp.int32, g_ref.shape[1:], 1)

    @pl.when(j == 0)
    def _():
        g_ref[...] = jnp.zeros(g_ref.shape, F32)

    qh = [q_ref[h] for h in range(N_HEADS)]
    for s in range(bps):
        kh = _head_blocks(page_refs[2 * s], page_refs[2 * s + 1])
        for h in range(N_HEADS):
            sc_ref[h, :, s * MOBA_BLOCK:(s + 1) * MOBA_BLOCK] = _dot(
                qh[h].astype(BF16), kh[h].astype(BF16), _NT)
        for h in range(N_HEADS):
            kmean = jnp.sum(kh[h], axis=0, keepdims=True) * (1.0 / MOBA_BLOCK)
            gcol = jnp.sum(qh[h] * kmean, axis=1, keepdims=True)
            g_ref[h] = jnp.where(lane == j * bps + s, gcol, g_ref[h])


def _paged_specs(page_table, page_shape, bps):
    npg = page_table.shape[1]
    ppb = MOBA_BLOCK // PAGE_SIZE
    assert ppb == 2
    return [pl.BlockSpec((None,) + page_shape,
                         functools.partial(lambda b, j, pt, o: (pt[b * npg + ppb * bps * j + o], 0, 0, 0), o=o))
            for o in range(ppb * bps)]


def _blocks_per_step(nbp):
    return 2 if nbp % 2 == 0 else 1


def sample_scores_call(page_table, q4, cache_k):
    bs, nh, rows, dh = q4.shape
    nbp = page_table.shape[1] * PAGE_SIZE // MOBA_BLOCK
    bps = _blocks_per_step(nbp)
    per_b = lambda last: pl.BlockSpec((None, nh, rows, last), lambda b, j, pt: (b, 0, 0, 0))
    gs = pltpu.PrefetchScalarGridSpec(
        num_scalar_prefetch=1, grid=(bs, nbp // bps),
        in_specs=[per_b(dh)] + _paged_specs(page_table, (PAGE_SIZE, nh, dh), bps),
        out_specs=[pl.BlockSpec((None, nh, rows, bps * MOBA_BLOCK), lambda b, j, pt: (b, 0, 0, j)), per_b(nbp)])
    return pl.pallas_call(
        functools.partial(_sample_scores_kernel, bps=bps), grid_spec=gs,
        out_shape=[jax.ShapeDtypeStruct((bs, nh, rows, nbp * MOBA_BLOCK), F32),
                   jax.ShapeDtypeStruct((bs, nh, rows, nbp), F32)],
        compiler_params=_cparams("parallel", "arbitrary"), name="sample_scores")(
            page_table.reshape(-1), q4, *([cache_k] * (2 * bps)))


def _sample_attend_kernel(pt_ref, sl_ref, sc_ref, g_ref, q_ref, ko_ref, vo_ref, *refs, nbp, bps, past_len):
    del pt_ref
    page_refs, (o_ref, m_scr, l_scr, acc_scr, sel_scr) = refs[:2 * bps], refs[2 * bps:]
    j = pl.program_id(1)
    rows = q_ref.shape[1]
    scale = 1.0 / math.sqrt(HEAD_DIM)
    tq = lax.broadcasted_iota(jnp.int32, (rows, 1), 0)
    heads = range(N_HEADS)

    @pl.when(j == 0)
    def _():
        lane = lax.broadcasted_iota(jnp.int32, (rows, nbp), 1)
        tk = lax.broadcasted_iota(jnp.int32, (rows, rows), 1)
        for h in heads:
            gate = g_ref[h]
            sel = jnp.zeros(gate.shape, jnp.bool_)
            for _ in range(MOBA_TOPK):
                mx = jnp.max(gate, axis=1, keepdims=True)
                idx = jnp.min(jnp.where(gate == mx, lane, nbp), axis=1, keepdims=True)
                pick = jnp.logical_and(lane == idx, mx > -jnp.inf)
                sel = jnp.logical_or(sel, pick)
                gate = jnp.where(pick, -jnp.inf, gate)
            sel_scr[h] = jnp.where(sel, 0.0, NEG)
            s = _dot(q_ref[h].astype(BF16), ko_ref[h].astype(BF16), _NT) * scale
            s = jnp.where(tk <= tq, s - sl_ref[h] * (tq - tk).astype(F32), NEG)
            m = jnp.max(s, axis=1, keepdims=True)
            pr = jnp.exp(s - m)
            m_scr[h] = m
            l_scr[h] = jnp.sum(pr, axis=1, keepdims=True)
            acc_scr[h] = _dot(pr.astype(BF16), vo_ref[h].astype(BF16))

    width = bps * MOBA_BLOCK
    lane_k = lax.broadcasted_iota(jnp.int32, (rows, width), 1)
    lane_b = lax.broadcasted_iota(jnp.int32, (rows, nbp), 1)
    dist = ((past_len + tq) - (j * width + lane_k)).astype(F32)
    vh = [[x.astype(BF16) for x in _head_blocks(page_refs[2 * sb], page_refs[2 * sb + 1])] for sb in range(bps)]
    prs, alphas = [], []
    for h in heads:
        sel = sel_scr[h]
        bias = jnp.sum(jnp.where(lane_b == j * bps, sel, 0.0), axis=1, keepdims=True)
        for sb in range(1, bps):
            bias_sb = jnp.sum(jnp.where(lane_b == j * bps + sb, sel, 0.0), axis=1, keepdims=True)
            bias = jnp.where(lane_k >= sb * MOBA_BLOCK, bias_sb, bias)
        s = sc_ref[h] * scale - sl_ref[h] * dist + bias
        m_old = m_scr[h]
        m_new = jnp.maximum(m_old, jnp.max(s, axis=1, keepdims=True))
        alpha = jnp.exp(m_old - m_new)
        pr = jnp.exp(s - m_new)
        m_scr[h] = m_new
        l_scr[h] = alpha * l_scr[h] + jnp.sum(pr, axis=1, keepdims=True)
        prs.append(pr.astype(BF16))
        alphas.append(alpha)
    for h in heads:
        pv = _dot(prs[h][:, :MOBA_BLOCK], vh[0][h])
        for sb in range(1, bps):
            pv = pv + _dot(prs[h][:, sb * MOBA_BLOCK:(sb + 1) * MOBA_BLOCK], vh[sb][h])
        acc_scr[h] = alphas[h] * acc_scr[h] + pv

    @pl.when(j == pl.num_programs(1) - 1)
    def _():
        for h in heads:
            o_ref[h] = acc_scr[h] / l_scr[h]


def sample_attend_call(page_table, slopes, scores, gate, q4, k_own, v_own, cache_v):
    bs, nh, rows, dh = q4.shape
    npg = page_table.shape[1]
    nbp = npg * PAGE_SIZE // MOBA_BLOCK
    bps = _blocks_per_step(nbp)
    per_b = lambda last: pl.BlockSpec((None, nh, rows, last), lambda b, j, pt: (b, 0, 0, 0))
    gs = pltpu.PrefetchScalarGridSpec(
        num_scalar_prefetch=1, grid=(bs, nbp // bps),
        in_specs=[pl.BlockSpec(memory_space=pltpu.SMEM),
                  pl.BlockSpec((None, nh, rows, bps * MOBA_BLOCK), lambda b, j, pt: (b, 0, 0, j)),
                  per_b(nbp), per_b(dh), per_b(dh), per_b(dh)]
                 + _paged_specs(page_table, (PAGE_SIZE, nh, dh), bps),
        out_specs=per_b(dh),
        scratch_shapes=[pltpu.VMEM((nh, rows, 1), F32), pltpu.VMEM((nh, rows, 1), F32),
                        pltpu.VMEM((nh, rows, dh), F32), pltpu.VMEM((nh, rows, nbp), F32)])
    return pl.pallas_call(
        functools.partial(_sample_attend_kernel, nbp=nbp, bps=bps, past_len=npg * PAGE_SIZE),
        grid_spec=gs, out_shape=jax.ShapeDtypeStruct((bs, nh, rows, dh), F32),
        compiler_params=_cparams("parallel", "arbitrary"), name="sample_attend")(
            page_table.reshape(-1), slopes, scores, gate, q4, k_own, v_own, *([cache_v] * (2 * bps)))


def _row_tile(t):
    for tm in (512, 256, 128):
        if t % tm == 0:
            return tm
    raise ValueError(f"token count {t} is not a multiple of 128")


def _ffn_tile(f):
    for nf in (1, 2, 4, 11, 22):
        if f % nf == 0 and (f // nf) % LANES == 0 and f // nf <= 1536:
            return f // nf
    raise ValueError(f"unsupported FFN width {f}")


def _trunk(x, p, wkv_in, shift_in, W, cache=None):
    b, t, d = x.shape
    nt = b * t
    tm = _row_tile(nt)
    h = x.reshape(nt, d)
    row = lambda v: v.reshape(1, -1)

    hn = rmsnorm_call(h, row(W['norm_mix'][0]), tm)
    hn3 = hn.reshape(b, t, d)
    xp = jnp.concatenate([shift_in[:, None, :], hn3[:, :-1]], axis=1).reshape(nt, d)
    shift_out = hn3[:, -1]
    r, k, v, lw, a, g = rwkv_proj_call(
        hn, xp, W['rwkv_mu'][0], W['w_r'], W['w_k'], W['w_v'], W['rwkv_w1'], W['rwkv_w2'],
        W['rwkv_a1'], W['rwkv_a2'], W['rwkv_g1'], W['rwkv_g2'],
        row(W['rwkv_w0'][0]), row(W['rwkv_a0'][0]), tm)
    chunk = SCAN_CHUNK if t % SCAN_CHUNK == 0 else 8
    tp = -(-t // chunk) * chunk
    seqs = (r, k, v, lw, a, g)
    if tp != t:
        seqs = tuple(jnp.pad(s.reshape(b, t, d), ((0, 0), (0, tp - t), (0, 0))).reshape(b * tp, d)
                     for s in seqs)
    o, zf = scan_call(*seqs, row(W['rwkv_k_k'][0]), row(W['rwkv_k_a'][0]), row(W['rwkv_r_k'][0]),
                      row(W['rwkv_lnx_w'][0]), row(W['rwkv_lnx_b'][0]), _state_to_z(wkv_in), b, tp, chunk,
                      d // LANES)
    if tp != t:
        o = o.reshape(b, tp, d)[:, :t].reshape(nt, d)
    wkv_out = _z_to_state(zf)
    h = matmul_res_call(o, W['rwkv_w_o'], h, tm)
    tf = _ffn_tile(W['ffn_w_gate'][0].shape[1])
    h = ffn_call(h, row(W['norm_ffn'][0]), W['ffn_w_gate'][0], W['ffn_w_up'][0], W['ffn_w_down'][0], tm, tf)
    h = ple_call(h, p[0].reshape(nt, -1), row(W['norm_ple'][0]), W['ple_w_gate'][0], W['ple_w_proj'][0], tm)

    slopes = jnp.exp2(-8.0 * jnp.arange(1, N_HEADS + 1, dtype=F32) / N_HEADS)
    if cache is None:
        k_new, v_new, q, kb, vt, kmean = kvq_call(
            h, row(W['norm_kv']), row(W['norm_mix'][1]), W['kv_w_k'], W['kv_w_v'], W['attn_w_q'],
            MOBA_BLOCK, wvt=W['kv_w_vT'], seq=t)
        attn = moba_prompt_call(slopes, q, kb, vt, kmean.reshape(b, t // MOBA_BLOCK, d), b, t)
    else:
        cache_k, cache_v, page_table = cache
        k_new, v_new, q = kvq_call(h, row(W['norm_kv']), row(W['norm_mix'][1]),
                                   W['kv_w_k'], W['kv_w_v'], W['attn_w_q'], tm)

        def per_head(z):
            z4 = jnp.swapaxes(z.reshape(b, t, N_HEADS, HEAD_DIM), 1, 2)
            return jnp.pad(z4, ((0, 0), (0, 0), (0, 8 - t), (0, 0)))

        q4 = per_head(q)
        scores, gate = sample_scores_call(page_table, q4, cache_k)
        attn4 = sample_attend_call(page_table, slopes, scores, gate, q4, per_head(k_new), per_head(v_new),
                                   cache_v)
        attn = jnp.swapaxes(attn4[:, :, :t], 1, 2).reshape(nt, d)
    h = matmul_res_call(attn, W['attn_w_o'], h, tm)
    h = ffn_call(h, row(W['norm_ffn'][1]), W['ffn_w_gate'][1], W['ffn_w_up'][1], W['ffn_w_down'][1], tm, tf)
    y = ple_call(h, p[1].reshape(nt, -1), row(W['norm_ple'][1]), W['ple_w_gate'][1], W['ple_w_proj'][1], tm,
                 g_final=row(W['norm_final']))
    shp = (b, t, N_HEADS, HEAD_DIM)
    return (y.reshape(b, t, d), wkv_out[None], shift_out[None], k_new.reshape(shp), v_new.reshape(shp))


def kernel(x_prompt, x_sample, p_prompt, p_sample, state_wkv, state_shift, cache_k, cache_v, page_table,
           norm_mix, norm_ffn, norm_ple, norm_kv, norm_final, rwkv_mu, rwkv_w_rkv, rwkv_w_o, rwkv_w0,
           rwkv_w1, rwkv_w2, rwkv_a0, rwkv_a1, rwkv_a2, rwkv_g1, rwkv_g2, rwkv_k_k, rwkv_k_a, rwkv_r_k,
           rwkv_lnx_w, rwkv_lnx_b, attn_w_q, attn_w_o, kv_w_k, kv_w_v, ffn_w_gate, ffn_w_up, ffn_w_down,
           ple_w_proj, ple_w_gate):
    assert norm_mix.shape[0] == 2 and state_wkv.shape[0] == 1, "one RWKV layer then one MoBA layer"
    bf = lambda w: w.astype(BF16)
    W = dict(norm_mix=norm_mix, norm_ffn=norm_ffn, norm_ple=norm_ple, norm_kv=norm_kv, norm_final=norm_final,
             rwkv_mu=rwkv_mu, w_r=bf(rwkv_w_rkv[0, 0]), w_k=bf(rwkv_w_rkv[0, 1]), w_v=bf(rwkv_w_rkv[0, 2]),
             rwkv_w_o=bf(rwkv_w_o[0]), rwkv_w0=rwkv_w0, rwkv_w1=bf(rwkv_w1[0]), rwkv_w2=bf(rwkv_w2[0]),
             rwkv_a0=rwkv_a0, rwkv_a1=bf(rwkv_a1[0]), rwkv_a2=bf(rwkv_a2[0]), rwkv_g1=bf(rwkv_g1[0]),
             rwkv_g2=bf(rwkv_g2[0]), rwkv_k_k=rwkv_k_k, rwkv_k_a=rwkv_k_a, rwkv_r_k=rwkv_r_k,
             rwkv_lnx_w=rwkv_lnx_w, rwkv_lnx_b=rwkv_lnx_b, attn_w_q=bf(attn_w_q[0]), attn_w_o=bf(attn_w_o[0]),
             kv_w_k=bf(kv_w_k), kv_w_v=bf(kv_w_v), kv_w_vT=bf(kv_w_v.T),
             ffn_w_gate=bf(ffn_w_gate), ffn_w_up=bf(ffn_w_up), ffn_w_down=bf(ffn_w_down),
             ple_w_proj=bf(ple_w_proj), ple_w_gate=bf(ple_w_gate))
    bp = x_prompt.shape[0]
    d = x_prompt.shape[2]
    wkv0 = jnp.zeros((bp, N_HEADS, HEAD_DIM, HEAD_DIM), F32)
    shift0 = jnp.zeros((bp, d), F32)
    y_p, wkv_p, shift_p, k_p, v_p = _trunk(x_prompt, p_prompt, wkv0, shift0, W)
    y_s, wkv_s, shift_s, k_s, v_s = _trunk(x_sample, p_sample, state_wkv[0], state_shift[0], W,
                                           cache=(cache_k, cache_v, page_table))
    return (y_p, y_s, wkv_p, shift_p, k_p, v_p, wkv_s, shift_s, k_s, v_s)
```

```python
import functools
import math

import jax
import jax.numpy as jnp
from jax import lax
from jax.experimental import pallas as pl
from jax.experimental.pallas import tpu as pltpu

F32, BF16 = jnp.float32, jnp.bfloat16
N_HEADS = 16
HEAD_DIM = 64
LANES = 128
MOBA_BLOCK = 256
MOBA_TOPK = 3
PAGE_SIZE = 128
RMS_EPS = 1e-6
LNX_EPS = 64e-5
NEG = -1e30
VMEM_LIMIT_BYTES = 56 * 1024 * 1024
SCAN_CHUNK = 64
POS_LANES = 3
VT_ROWS = 80


def _cparams(*sem):
    return pltpu.CompilerParams(dimension_semantics=sem, vmem_limit_bytes=VMEM_LIMIT_BYTES)


def _rms(x, g):
    return x * lax.rsqrt(jnp.mean(x * x, axis=-1, keepdims=True) + RMS_EPS) * g


def _sigmoid(x):
    return 1.0 / (1.0 + jnp.exp(-x))


_NN = (((1,), (0,)), ((), ()))
_NT = (((1,), (1,)), ((), ()))


def _dot(a, b, dims=_NN):
    return lax.dot_general(a, b, dims, preferred_element_type=F32)


def _bf(x):
    return x.astype(BF16)


def _bdot(a, w):
    return _dot(a.astype(BF16), w)


def _split2(x):
    hi = x.astype(BF16)
    lo = (x - hi.astype(F32)).astype(BF16)
    return hi, lo


def _dot3(a, b, dims=_NN):
    ah, al = _split2(a)
    bh, bl = _split2(b)
    return _dot(ah, bh, dims) + (_dot(ah, bl, dims) + _dot(al, bh, dims))


def _full(shape):
    n = len(shape)
    return pl.BlockSpec(shape, lambda *_: (0,) * n)


def _rms_kernel(x_ref, g_ref, o_ref):
    o_ref[...] = _rms(x_ref[...], g_ref[...])


def rmsnorm_call(x, g, tm):
    t, d = x.shape
    row = pl.BlockSpec((tm, d), lambda i: (i, 0))
    return pl.pallas_call(
        _rms_kernel, grid=(t // tm,), in_specs=[row, _full((1, d))], out_specs=row,
        out_shape=jax.ShapeDtypeStruct((t, d), F32), compiler_params=_cparams("parallel"),
        name="rmsnorm")(x, g)


def _rwkv_proj_kernel(hn_ref, xp_ref, mu_ref, wr, wk, wv, w1, w2, a1, a2, g1, g2, w0, a0,
                      r_o, k_o, v_o, lw_o, a_o, g_o):
    hn = hn_ref[...]
    xx = xp_ref[...] - hn

    def mix(i):
        return hn + xx * mu_ref[i:i + 1, :]

    r_o[...] = _bdot(mix(0), wr[...])
    k_o[...] = _bdot(mix(2), wk[...])
    v_o[...] = _bdot(mix(3), wv[...])
    u = w0[...] + _bdot(jnp.tanh(_bdot(mix(1), w1[...])), w2[...])
    lw_o[...] = -math.exp(-0.5) * _sigmoid(u)
    a_o[...] = _sigmoid(a0[...] + _bdot(_bdot(mix(4), a1[...]), a2[...]))
    g_o[...] = _bdot(_sigmoid(_bdot(mix(5), g1[...])), g2[...])


def rwkv_proj_call(hn, xp, mu, wr, wk, wv, w1, w2, a1, a2, g1, g2, w0, a0, tm):
    t, d = hn.shape
    row = pl.BlockSpec((tm, d), lambda i: (i, 0))
    ws = [mu, wr, wk, wv, w1, w2, a1, a2, g1, g2, w0, a0]
    return pl.pallas_call(
        _rwkv_proj_kernel, grid=(t // tm,),
        in_specs=[row, row] + [_full(w.shape) for w in ws],
        out_specs=[row] * 6, out_shape=[jax.ShapeDtypeStruct((t, d), F32)] * 6,
        compiler_params=_cparams("parallel"), name="rwkv_proj")(hn, xp, *ws)


def _matmul_res_kernel(x_ref, w_ref, res_ref, o_ref):
    o_ref[...] = res_ref[...] + _bdot(x_ref[...], w_ref[...])


def matmul_res_call(x, w, res, tm):
    t, d = x.shape
    row = pl.BlockSpec((tm, d), lambda i: (i, 0))
    return pl.pallas_call(
        _matmul_res_kernel, grid=(t // tm,), in_specs=[row, _full(w.shape), row], out_specs=row,
        out_shape=jax.ShapeDtypeStruct((t, d), F32), compiler_params=_cparams("parallel"),
        name="matmul_res")(x, w, res)


def _ffn_kernel(h_ref, g_ref, wg_ref, wu_ref, wd_ref, o_ref, xn_scr):
    @pl.when(pl.program_id(1) == 0)
    def _():
        h = h_ref[...]
        xn_scr[...] = _rms(h, g_ref[...]).astype(BF16)
        o_ref[...] = h

    xn = xn_scr[...]
    gt = _dot(xn, wg_ref[...])
    up = _dot(xn, wu_ref[...])
    act = gt * _sigmoid(gt) * up
    o_ref[...] += _bdot(act, wd_ref[...])


def ffn_call(h, g, wg, wu, wd, tm, tf):
    t, d = h.shape
    f = wg.shape[1]
    row = pl.BlockSpec((tm, d), lambda i, j: (i, 0))
    return pl.pallas_call(
        _ffn_kernel, grid=(t // tm, f // tf),
        in_specs=[row, pl.BlockSpec((1, d), lambda i, j: (0, 0)),
                  pl.BlockSpec((d, tf), lambda i, j: (0, j)),
                  pl.BlockSpec((d, tf), lambda i, j: (0, j)),
                  pl.BlockSpec((tf, d), lambda i, j: (j, 0))],
        out_specs=row, out_shape=jax.ShapeDtypeStruct((t, d), F32),
        scratch_shapes=[pltpu.VMEM((tm, d), BF16)],
        compiler_params=_cparams("parallel", "arbitrary"), name="ffn")(h, g, wg, wu, wd)


def _ple_kernel(h_ref, p_ref, g_ref, wg_ref, wp_ref, *rest, final):
    h = h_ref[...]
    gate = _sigmoid(_bdot(_rms(h, g_ref[...]), wg_ref[...]))
    hn = h + _bdot(p_ref[...], wp_ref[...]) * gate
    if final:
        gf_ref, y_ref = rest
        y_ref[...] = _rms(hn, gf_ref[...])
    else:
        rest[0][...] = hn


def ple_call(h, p, g, wg, wp, tm, g_final=None):
    t, d = h.shape
    row = pl.BlockSpec((tm, d), lambda i: (i, 0))
    prow = pl.BlockSpec((tm, p.shape[1]), lambda i: (i, 0))
    ins = [h, p, g, wg, wp]
    specs = [row, prow, _full(g.shape), _full(wg.shape), _full(wp.shape)]
    if g_final is not None:
        ins.append(g_final)
        specs.append(_full(g_final.shape))
    return pl.pallas_call(
        functools.partial(_ple_kernel, final=g_final is not None), grid=(t // tm,),
        in_specs=specs, out_specs=row, out_shape=jax.ShapeDtypeStruct((t, d), F32),
        compiler_params=_cparams("parallel"), name="ple")(*ins)


def _kvq_kernel(h_ref, gkv_ref, gq_ref, wk_ref, wv_ref, wq_ref, *rest, prompt):
    h = h_ref[...]
    hk = _rms(h, gkv_ref[...]).astype(BF16)
    k = _dot(hk, wk_ref[...])
    if not prompt:
        k_o, v_o, q_o = rest
        k_o[...] = k
        v_o[...] = _dot(hk, wv_ref[...])
        q_o[...] = _bdot(_rms(h, gq_ref[...]), wq_ref[...])
        return
    wkt_ref, kt_o, vt_o, q_o, kb_o, vtb_o, km_o = rest
    q_o[...] = _bdot(_rms(h, gq_ref[...]), wq_ref[...])
    kt_o[...] = _dot(wkt_ref[...], hk, _NT)
    vt = _dot(wv_ref[...], hk, _NT)
    vt_o[...] = vt
    tm, d = k.shape
    lane = lax.broadcasted_iota(jnp.int32, (tm, LANES), 1)
    rpos = lax.broadcasted_iota(jnp.int32, (tm, LANES), 0).astype(F32)
    for p in range(d // LANES):
        kp = k[:, p * LANES:(p + 1) * LANES]
        for hh in range(2):
            own = (lane < HEAD_DIM) == (hh == 0)
            pos = lane - (HEAD_DIM if hh == 0 else 0)
            extra = jnp.where(jnp.logical_and(pos >= 0, pos < POS_LANES), rpos, 0.0)
            c0 = (2 * p + hh) * LANES
            kb_o[:, c0:c0 + LANES] = jnp.where(own, kp, extra).astype(BF16)
    ones_rows = jnp.where(lax.broadcasted_iota(jnp.int32, (VT_ROWS - HEAD_DIM, tm), 0) == 0, 1.0, 0.0)
    for hd in range(d // HEAD_DIM):
        vtb_o[hd * VT_ROWS:hd * VT_ROWS + HEAD_DIM, :] = vt[hd * HEAD_DIM:(hd + 1) * HEAD_DIM, :].astype(BF16)
        vtb_o[hd * VT_ROWS + HEAD_DIM:(hd + 1) * VT_ROWS, :] = ones_rows.astype(BF16)
    km_o[...] = jnp.sum(k, axis=0, keepdims=True) * (1.0 / MOBA_BLOCK)


def kvq_call(h, gkv, gq, wk, wv, wq, tm, wkt=None, seq=None):
    t, d = h.shape
    prompt = wkt is not None
    row = pl.BlockSpec((tm, d), lambda i: (i, 0))
    ins = [h, gkv, gq, wk, wv, wq]
    specs = [row, _full(gkv.shape), _full(gq.shape), _full(wk.shape), _full(wv.shape), _full(wq.shape)]
    if prompt:
        assert tm == MOBA_BLOCK
        tps = seq // tm
        nh = d // HEAD_DIM
        tspec = lambda rows: pl.BlockSpec((None, rows, tm), lambda i: (i // tps, 0, i % tps))
        ins.append(wkt)
        specs.append(_full(wkt.shape))
        outs = [jax.ShapeDtypeStruct((t // seq, d, seq), F32)] * 2 + [
            jax.ShapeDtypeStruct((t, d), F32),
            jax.ShapeDtypeStruct((t, nh * LANES), BF16),
            jax.ShapeDtypeStruct((t // seq, nh * VT_ROWS, seq), BF16),
            jax.ShapeDtypeStruct((t // tm, 1, d), F32)]
        ospecs = [tspec(d), tspec(d), row, pl.BlockSpec((tm, nh * LANES), lambda i: (i, 0)),
                  tspec(nh * VT_ROWS), pl.BlockSpec((None, 1, d), lambda i: (i, 0, 0))]
    else:
        outs = [jax.ShapeDtypeStruct((t, d), F32)] * 3
        ospecs = [row] * 3
    return pl.pallas_call(
        functools.partial(_kvq_kernel, prompt=prompt), grid=(t // tm,), in_specs=specs,
        out_specs=ospecs, out_shape=outs, compiler_params=_cparams("parallel"), name="kvq")(*ins)


def _scan_kernel(r_ref, k_ref, v_ref, lw_ref, a_ref, g_ref, kk_ref, ka_ref, rk_ref, lnw_ref, lnb_ref,
                 z0_ref, o_ref, zout_ref, z_scr, *, C, NP):
    c = pl.program_id(2)

    @pl.when(c == 0)
    def _():
        z_scr[...] = z0_ref[...]

    lane = lax.broadcasted_iota(jnp.int32, (1, LANES), 1)
    m0 = lane < HEAD_DIM
    masks = (m0, jnp.logical_not(m0))
    pairs = range(NP)
    heads = [(p, h) for p in pairs for h in range(2)]

    def hsum(x):
        s0 = jnp.sum(jnp.where(m0, x, 0.0), axis=-1, keepdims=True)
        s1 = jnp.sum(jnp.where(m0, 0.0, x), axis=-1, keepdims=True)
        return jnp.where(m0, s0, s1)

    def tile(ref, p):
        return ref[:, p * LANES:(p + 1) * LANES]

    ri = lax.broadcasted_iota(jnp.int32, (C, C), 0)
    ci = lax.broadcasted_iota(jnp.int32, (C, C), 1)
    ltri = jnp.where(ci <= ri, 1.0, 0.0).astype(BF16)
    strict_c = ci < ri
    incl_c = ci <= ri
    rw = lax.broadcasted_iota(jnp.int32, (C, 2 * C), 0)
    cw = lax.broadcasted_iota(jnp.int32, (C, 2 * C), 1)
    strict_w = jnp.where(cw < C, cw, cw - C) < rw
    right_w = cw >= C
    eye_w = cw - C == rw
    zero_c = jnp.zeros((C, LANES), BF16)
    rz = lax.broadcasted_iota(jnp.int32, (LANES, LANES), 0)
    cz = lax.broadcasted_iota(jnp.int32, (LANES, LANES), 1)
    same_head = (rz < HEAD_DIM) == (cz < HEAD_DIM)

    r = [tile(r_ref, p) for p in pairs]
    k = [tile(k_ref, p) for p in pairs]
    v = [tile(v_ref, p) for p in pairs]
    lw = [tile(lw_ref, p) for p in pairs]
    a = [tile(a_ref, p) for p in pairs]

    def cumsum(x):
        l1 = x.astype(BF16)
        rem = x - l1.astype(F32)
        l2 = rem.astype(BF16)
        l3 = (rem - l2.astype(F32)).astype(BF16)
        return _dot(ltri, l1) + (_dot(ltri, l2) + _dot(ltri, l3))

    cl = [cumsum(lw[p]) for p in pairs]
    z = [z_scr[p] for p in pairs]
    zs = [_bf(z[p]) for p in pairs]

    kmod, ars, rhs_cols, bk_t, gc_col, vs = [], [], [], [], [], []
    for p in pairs:
        kkr = k[p] * tile(kk_ref, p)
        kk = kkr / jnp.maximum(jnp.sqrt(hsum(kkr * kkr)), 1e-12)
        beta = kk * a[p]
        km = k[p] * (1.0 + (a[p] - 1.0) * tile(ka_ref, p))
        kmod.append(km)
        igam = jnp.exp(-cl[p])
        a_t = -kk * jnp.exp(cl[p] - lw[p])
        r_t = r[p] * jnp.exp(cl[p])
        b_h = beta * igam
        k_h = km * igam
        gout = jnp.exp(cl[p][C - 1:C, :] - cl[p])
        ars.append(_bf(jnp.concatenate([a_t, r_t], axis=0)))
        rhs_cols.append(_bf(jnp.concatenate([b_h, b_h, k_h], axis=0)))
        bk_t.append(_bf(jnp.concatenate([(beta * gout).T, (km * gout).T], axis=1)))
        gc_col.append(jnp.exp(jnp.sum(lw[p].T, axis=1, keepdims=True)))
        vs.append(_bf(v[p]))

    xs = [_dot(ars[p], zs[p]) for p in pairs]
    gfull = [_dot(jnp.where(masks[h], ars[p], jnp.zeros_like(ars[p])), rhs_cols[p], _NT)
             for p, h in heads]
    w = [jnp.where(strict_w, g[:C, :2 * C], 0.0) for g in gfull]
    gk = [_bf(jnp.concatenate([jnp.where(strict_c, g[:C, 2 * C:], 0.0),
                               jnp.where(incl_c, g[C:, 2 * C:], 0.0)], axis=0)) for g in gfull]
    arb = [_bf(jnp.where(incl_c, g[C:, :C], 0.0)) for g in gfull]
    av = [_dot(gk[i], vs[p]) for i, (p, h) in enumerate(heads)]
    avs = [jnp.where(m0, av[2 * p], av[2 * p + 1]) for p in pairs]

    for _ in range(int(math.log2(C))):
        wb = [_bf(x) for x in w]
        w = [jnp.where(right_w, w[i], 0.0) + _dot(wb[i][:, :C], wb[i]) for i in range(len(heads))]
    minv = [_bf(jnp.where(right_w, x, 0.0) + jnp.where(eye_w, 1.0, 0.0)) for x in w]

    rhs = [jnp.concatenate([zero_c, _bf(xs[p][:C] + avs[p][:C])], axis=0) for p in pairs]
    uh = [_dot(minv[i], rhs[p]) for i, (p, h) in enumerate(heads)]
    u = [jnp.where(m0, uh[2 * p], uh[2 * p + 1]) for p in pairs]
    us = [_bf(u[p]) for p in pairs]
    oh = [_dot(arb[i], us[p]) for i, (p, h) in enumerate(heads)]
    zd = [_dot(bk_t[p], jnp.concatenate([us[p], vs[p]], axis=0)) for p in pairs]

    inv_n = 1.0 / HEAD_DIM
    for p in pairs:
        z_new = jnp.where(same_head, z[p] * gc_col[p] + zd[p], 0.0)
        z_scr[p] = z_new

        @pl.when(c == pl.num_programs(2) - 1)
        def _():
            zout_ref[p] = z_new

        o = xs[p][C:] + avs[p][C:] + jnp.where(m0, oh[2 * p], oh[2 * p + 1])
        mean = hsum(o) * inv_n
        dlt = o - mean
        var = hsum(dlt * dlt) * inv_n
        on = dlt * lax.rsqrt(var + LNX_EPS) * tile(lnw_ref, p) + tile(lnb_ref, p)
        bonus = hsum(r[p] * kmod[p] * tile(rk_ref, p)) * v[p]
        o_ref[:, p * LANES:(p + 1) * LANES] = (on + bonus) * tile(g_ref, p)


def scan_call(r, k, v, lw, a, g, k_k, k_a, r_k, lnw, lnb, z0, nb, seq, C, NP):
    t, d = r.shape
    npairs = d // LANES
    nc = seq // C
    w = NP * LANES
    row = pl.BlockSpec((C, w), lambda b, p, c: (b * nc + c, p))
    par = pl.BlockSpec((1, w), lambda b, p, c: (0, p))
    zspec = pl.BlockSpec((None, NP, LANES, LANES), lambda b, p, c: (b, p, 0, 0))
    return pl.pallas_call(
        functools.partial(_scan_kernel, C=C, NP=NP), grid=(nb, npairs // NP, nc),
        in_specs=[row] * 6 + [par] * 5 + [zspec],
        out_specs=[row, zspec],
        out_shape=[jax.ShapeDtypeStruct((t, d), F32),
                   jax.ShapeDtypeStruct((nb, npairs, LANES, LANES), F32)],
        scratch_shapes=[pltpu.VMEM((NP, LANES, LANES), F32)],
        compiler_params=_cparams("parallel", "parallel", "arbitrary"), name="rwkv_scan")(
            r, k, v, lw, a, g, k_k, k_a, r_k, lnw, lnb, z0)


def _state_to_z(s):
    b, h = s.shape[:2]
    st = jnp.swapaxes(s, 2, 3).reshape(b, h // 2, 2, HEAD_DIM, 1, HEAD_DIM)
    eye = jnp.eye(2, dtype=F32).reshape(1, 1, 2, 1, 2, 1)
    return (st * eye).reshape(b, h // 2, LANES, LANES)


def _z_to_state(z):
    b, hp = z.shape[:2]
    z6 = z.reshape(b, hp, 2, HEAD_DIM, 2, HEAD_DIM)
    st = jnp.stack([z6[:, :, 0, :, 0, :], z6[:, :, 1, :, 1, :]], axis=2)
    return jnp.swapaxes(st.reshape(b, hp * 2, HEAD_DIM, HEAD_DIM), 2, 3)


def _top3_rows(gate, rowi, nrow):
    sel = jnp.zeros(gate.shape, jnp.bool_)
    for _ in range(MOBA_TOPK):
        mx = jnp.max(gate, axis=0, keepdims=True)
        idx = jnp.min(jnp.where(gate == mx, rowi, nrow), axis=0, keepdims=True)
        pick = jnp.logical_and(rowi == idx, mx > -jnp.inf)
        sel = jnp.logical_or(sel, pick)
        gate = jnp.where(pick, -jnp.inf, gate)
    return sel


def _moba_prompt_kernel(sl_ref, q_ref, k_ref, vt_ref, km_ref, o_ref, sel_scr, sa_scr, sb_scr, pa_scr, pb_scr,
                        *, nb):
    p = pl.program_id(1)
    i = pl.program_id(2)
    blk = MOBA_BLOCK
    lane = lax.broadcasted_iota(jnp.int32, (1, LANES), 1)
    m0 = lane < HEAD_DIM
    masks = (m0, jnp.logical_not(m0))
    q = q_ref[...]
    km = km_ref[...]
    rowi = lax.broadcasted_iota(jnp.int32, (nb, blk), 0)
    causal = (lax.broadcasted_iota(jnp.int32, (blk, blk), 1)
              >= lax.broadcasted_iota(jnp.int32, (blk, blk), 0))
    log2e = 1.0 / math.log(2.0)
    scale2 = log2e / math.sqrt(HEAD_DIM)

    qb, slopes = [], []
    for h in range(2):
        qh = jnp.where(masks[h], q, 0.0)
        gate = jnp.where(rowi < i, _dot3(km, qh, _NT), -jnp.inf)
        sel_scr[h] = jnp.where(_top3_rows(gate, rowi, nb), 0.0, NEG)
        slope = sl_ref[2 * p + h] * log2e
        slopes.append(slope)
        pos = lane - (HEAD_DIM if h == 0 else 0)
        rest = jnp.full((1, LANES), slope, F32)
        terms = jnp.zeros((1, LANES), F32)
        for t in range(POS_LANES):
            part = rest.astype(BF16).astype(F32)
            terms = jnp.where(pos == t, part, terms)
            rest = rest - part
        qb.append(jnp.where(masks[h], q * scale2, terms).astype(BF16))

    def block_off(jb):
        return pl.multiple_of(jnp.minimum(jb, nb - 1) * blk, blk)

    def scores_into(dst, jb):
        kt = k_ref[pl.ds(block_off(jb), blk), :]
        for h in range(2):
            dst[h] = _dot(kt[:, h * LANES:(h + 1) * LANES], qb[h], _NT)

    def pv_of(p_src, jb):
        vt = vt_ref[:, pl.ds(block_off(jb), blk)]
        return [_dot(vt[h * VT_ROWS:(h + 1) * VT_ROWS, :], p_src[h]) for h in range(2)]

    def step(s_src, p_dst, rowvec, pend, carry, own):
        pv = pv_of(*pend)
        new = []
        for h in range(2):
            m, a_prev, acc = carry[h]
            s = s_src[h]
            if own:
                s = jnp.where(causal, s, NEG)
            m_new = jnp.maximum(m, jnp.max(s, axis=0, keepdims=True) + rowvec[h])
            p_dst[h] = jnp.exp2(s - (m_new - rowvec[h])).astype(BF16)
            new.append((m_new, jnp.exp2(m - m_new), a_prev * acc + pv[h]))
        return tuple(new)

    def past_rowvec(jb):
        gap = ((i - jb) * blk).astype(F32)
        jc = jnp.minimum(jb, nb - 1)
        return [jnp.where(jb < i, sel_scr[h, pl.ds(jc, 1), :] - slopes[h] * gap, NEG) for h in range(2)]

    scores_into(sb_scr, i)
    scores_into(sa_scr, 0)
    pa_scr[...] = jnp.zeros(pa_scr.shape, BF16)
    zero_row = jnp.zeros((1, blk), F32)
    init = tuple((jnp.full((1, blk), NEG, F32), zero_row, jnp.zeros((VT_ROWS, blk), F32)) for _ in range(2))
    carry = step(sb_scr, pb_scr, (zero_row, zero_row), (pa_scr, 0), init, True)

    def body(jj, carry):
        j0 = 2 * jj
        scores_into(sb_scr, j0 + 1)
        carry = step(sa_scr, pa_scr, past_rowvec(j0), (pb_scr, jnp.where(jj == 0, i, j0 - 1)), carry, False)
        scores_into(sa_scr, j0 + 2)
        return step(sb_scr, pb_scr, past_rowvec(j0 + 1), (pa_scr, j0), carry, False)

    trips = (i + 1) // 2
    carry = lax.fori_loop(0, trips, body, carry)
    pv = pv_of(pb_scr, jnp.where(trips == 0, i, 2 * trips - 1))
    acc = [carry[h][1] * carry[h][2] + pv[h] for h in range(2)]
    out_t = jnp.concatenate([acc[h][:HEAD_DIM] / acc[h][HEAD_DIM:HEAD_DIM + 1] for h in range(2)], axis=0)
    o_ref[...] = out_t.T


def moba_prompt_call(slopes, q, kb, vt, kmean, nbatch, seq):
    t, d = q.shape
    nb = seq // MOBA_BLOCK
    npairs = d // LANES
    qspec = pl.BlockSpec((MOBA_BLOCK, LANES), lambda b, p, i: (b * nb + i, p))
    tile = pltpu.VMEM((2, MOBA_BLOCK, MOBA_BLOCK), F32)
    ptile = pltpu.VMEM((2, MOBA_BLOCK, MOBA_BLOCK), BF16)
    return pl.pallas_call(
        functools.partial(_moba_prompt_kernel, nb=nb), grid=(nbatch, npairs, nb),
        in_specs=[pl.BlockSpec(memory_space=pltpu.SMEM), qspec,
                  pl.BlockSpec((seq, 2 * LANES), lambda b, p, i: (b, p)),
                  pl.BlockSpec((None, 2 * VT_ROWS, seq), lambda b, p, i: (b, p, 0)),
                  pl.BlockSpec((None, nb, LANES), lambda b, p, i: (b, 0, p))],
        out_specs=qspec, out_shape=jax.ShapeDtypeStruct((t, d), F32),
        scratch_shapes=[pltpu.VMEM((2, nb, MOBA_BLOCK), F32), tile, tile, ptile, ptile],
        compiler_params=_cparams("parallel", "parallel", "arbitrary"), name="moba_prompt")(
            slopes, q, kb, vt, kmean)


def _head_tiles(page_refs, h):
    return jnp.concatenate([ref[h] for ref in page_refs], axis=1).astype(BF16)


def _sample_scores_kernel(pt_ref, q_ref, *refs, bps):
    del pt_ref
    page_refs, (sc_ref, g_ref) = refs[:2 * bps], refs[2 * bps:]
    j = pl.program_id(1)
    rows = q_ref.shape[1]
    lane = lax.broadcasted_iota(jnp.int32, g_ref.shape[1:], 1)

    @pl.when(j == 0)
    def _():
        g_ref[...] = jnp.zeros(g_ref.shape, F32)

    for h in range(N_HEADS):
        q = q_ref[h]
        q1 = q.astype(BF16)
        r1 = q - q1.astype(F32)
        q2 = r1.astype(BF16)
        q3 = (r1 - q2.astype(F32)).astype(BF16)
        s3 = _dot(jnp.concatenate([q1, q2, q3], axis=0), _head_tiles(page_refs, h))
        sc_ref[h] = s3[:rows]
        full = s3[:rows] + (s3[rows:2 * rows] + s3[2 * rows:])
        g = g_ref[h]
        for s in range(bps):
            gcol = jnp.sum(full[:, s * MOBA_BLOCK:(s + 1) * MOBA_BLOCK], axis=1, keepdims=True)
            g = jnp.where(lane == j * bps + s, gcol * (1.0 / MOBA_BLOCK), g)
        g_ref[h] = g


def _paged_specs(page_table, page_shape, bps):
    npg = page_table.shape[1]
    ppb = MOBA_BLOCK // PAGE_SIZE
    assert ppb == 2
    return [pl.BlockSpec((None,) + page_shape,
                         functools.partial(lambda b, j, pt, o: (pt[b * npg + ppb * bps * j + o], 0, 0, 0), o=o))
            for o in range(ppb * bps)]


def _blocks_per_step(nbp):
    return 2 if nbp % 2 == 0 else 1


def sample_scores_call(page_table, q4, cache_kt):
    bs, nh, rows, dh = q4.shape
    nbp = page_table.shape[1] * PAGE_SIZE // MOBA_BLOCK
    bps = _blocks_per_step(nbp)
    per_b = lambda last: pl.BlockSpec((None, nh, rows, last), lambda b, j, pt: (b, 0, 0, 0))
    gs = pltpu.PrefetchScalarGridSpec(
        num_scalar_prefetch=1, grid=(bs, nbp // bps),
        in_specs=[per_b(dh)] + _paged_specs(page_table, (nh, dh, PAGE_SIZE), bps),
        out_specs=[pl.BlockSpec((None, nh, rows, bps * MOBA_BLOCK), lambda b, j, pt: (b, 0, 0, j)), per_b(nbp)])
    return pl.pallas_call(
        functools.partial(_sample_scores_kernel, bps=bps), grid_spec=gs,
        out_shape=[jax.ShapeDtypeStruct((bs, nh, rows, nbp * MOBA_BLOCK), F32),
                   jax.ShapeDtypeStruct((bs, nh, rows, nbp), F32)],
        compiler_params=_cparams("parallel", "arbitrary"), name="sample_scores")(
            page_table.reshape(-1), q4, *([cache_kt] * (2 * bps)))


def _sample_attend_kernel(pt_ref, sl_ref, sc_ref, g_ref, q_ref, ko_ref, vo_ref, *refs, nbp, bps, past_len):
    del pt_ref
    page_refs, (o_ref, m_scr, l_scr, acc_scr, sel_scr) = refs[:2 * bps], refs[2 * bps:]
    j = pl.program_id(1)
    rows = q_ref.shape[1]
    scale = 1.0 / math.sqrt(HEAD_DIM)
    tq = lax.broadcasted_iota(jnp.int32, (rows, 1), 0)
    heads = range(N_HEADS)

    @pl.when(j == 0)
    def _():
        lane = lax.broadcasted_iota(jnp.int32, (rows, nbp), 1)
        tk = lax.broadcasted_iota(jnp.int32, (rows, rows), 1)
        for h in heads:
            gate = g_ref[h]
            sel = jnp.zeros(gate.shape, jnp.bool_)
            for _ in range(MOBA_TOPK):
                mx = jnp.max(gate, axis=1, keepdims=True)
                idx = jnp.min(jnp.where(gate == mx, lane, nbp), axis=1, keepdims=True)
                pick = jnp.logical_and(lane == idx, mx > -jnp.inf)
                sel = jnp.logical_or(sel, pick)
                gate = jnp.where(pick, -jnp.inf, gate)
            sel_scr[h] = jnp.where(sel, 0.0, NEG)
            s = _dot(q_ref[h].astype(BF16), ko_ref[h].astype(BF16), _NT) * scale
            s = jnp.where(tk <= tq, s - sl_ref[h] * (tq - tk).astype(F32), NEG)
            m = jnp.max(s, axis=1, keepdims=True)
            pr = jnp.exp(s - m)
            m_scr[h] = m
            l_scr[h] = jnp.sum(pr, axis=1, keepdims=True)
            acc_scr[h] = _dot(pr.astype(BF16), vo_ref[h].astype(BF16))

    width = bps * MOBA_BLOCK
    lane_k = lax.broadcasted_iota(jnp.int32, (rows, width), 1)
    lane_b = lax.broadcasted_iota(jnp.int32, (rows, nbp), 1)
    dist = ((past_len + tq) - (j * width + lane_k)).astype(F32)
    prs, alphas = [], []
    for h in heads:
        sel = sel_scr[h]
        bias = jnp.sum(jnp.where(lane_b == j * bps, sel, 0.0), axis=1, keepdims=True)
        for sb in range(1, bps):
            bias_sb = jnp.sum(jnp.where(lane_b == j * bps + sb, sel, 0.0), axis=1, keepdims=True)
            bias = jnp.where(lane_k >= sb * MOBA_BLOCK, bias_sb, bias)
        s = sc_ref[h] * scale - sl_ref[h] * dist + bias
        m_old = m_scr[h]
        m_new = jnp.maximum(m_old, jnp.max(s, axis=1, keepdims=True))
        alpha = jnp.exp(m_old - m_new)
        pr = jnp.exp(s - m_new)
        m_scr[h] = m_new
        l_scr[h] = alpha * l_scr[h] + jnp.sum(pr, axis=1, keepdims=True)
        prs.append(pr.astype(BF16))
        alphas.append(alpha)
    for h in heads:
        acc_scr[h] = alphas[h] * acc_scr[h] + _dot(prs[h], _head_tiles(page_refs, h), _NT)

    @pl.when(j == pl.num_programs(1) - 1)
    def _():
        for h in heads:
            o_ref[h] = acc_scr[h] / l_scr[h]


def sample_attend_call(page_table, slopes, scores, gate, q4, k_own, v_own, cache_vt):
    bs, nh, rows, dh = q4.shape
    npg = page_table.shape[1]
    nbp = npg * PAGE_SIZE // MOBA_BLOCK
    bps = _blocks_per_step(nbp)
    per_b = lambda last: pl.BlockSpec((None, nh, rows, last), lambda b, j, pt: (b, 0, 0, 0))
    gs = pltpu.PrefetchScalarGridSpec(
        num_scalar_prefetch=1, grid=(bs, nbp // bps),
        in_specs=[pl.BlockSpec(memory_space=pltpu.SMEM),
                  pl.BlockSpec((None, nh, rows, bps * MOBA_BLOCK), lambda b, j, pt: (b, 0, 0, j)),
                  per_b(nbp), per_b(dh), per_b(dh), per_b(dh)]
                 + _paged_specs(page_table, (nh, dh, PAGE_SIZE), bps),
        out_specs=per_b(dh),
        scratch_shapes=[pltpu.VMEM((nh, rows, 1), F32), pltpu.VMEM((nh, rows, 1), F32),
                        pltpu.VMEM((nh, rows, dh), F32), pltpu.VMEM((nh, rows, nbp), F32)])
    return pl.pallas_call(
        functools.partial(_sample_attend_kernel, nbp=nbp, bps=bps, past_len=npg * PAGE_SIZE),
        grid_spec=gs, out_shape=jax.ShapeDtypeStruct((bs, nh, rows, dh), F32),
        compiler_params=_cparams("parallel", "arbitrary"), name="sample_attend")(
            page_table.reshape(-1), slopes, scores, gate, q4, k_own, v_own, *([cache_vt] * (2 * bps)))


def _row_tile(t):
    for tm in (512, 256, 128):
        if t % tm == 0:
            return tm
    raise ValueError(f"token count {t} is not a multiple of 128")


def _ffn_tile(f):
    for nf in (1, 2, 4, 11, 22):
        if f % nf == 0 and (f // nf) % LANES == 0 and f // nf <= 1536:
            return f // nf
    raise ValueError(f"unsupported FFN width {f}")


def _trunk(x, p, wkv_in, shift_in, W, cache=None):
    b, t, d = x.shape
    nt = b * t
    tm = _row_tile(nt)
    h = x.reshape(nt, d)
    row = lambda v: v.reshape(1, -1)

    hn = rmsnorm_call(h, row(W['norm_mix'][0]), tm)
    hn3 = hn.reshape(b, t, d)
    xp = jnp.concatenate([shift_in[:, None, :], hn3[:, :-1]], axis=1).reshape(nt, d)
    shift_out = hn3[:, -1]
    r, k, v, lw, a, g = rwkv_proj_call(
        hn, xp, W['rwkv_mu'][0], W['w_r'], W['w_k'], W['w_v'], W['rwkv_w1'], W['rwkv_w2'],
        W['rwkv_a1'], W['rwkv_a2'], W['rwkv_g1'], W['rwkv_g2'],
        row(W['rwkv_w0'][0]), row(W['rwkv_a0'][0]), tm)
    chunk = SCAN_CHUNK if t % SCAN_CHUNK == 0 else 8
    tp = -(-t // chunk) * chunk
    seqs = (r, k, v, lw, a, g)
    if tp != t:
        seqs = tuple(jnp.pad(s.reshape(b, t, d), ((0, 0), (0, tp - t), (0, 0))).reshape(b * tp, d)
                     for s in seqs)
    o, zf = scan_call(*seqs, row(W['rwkv_k_k'][0]), row(W['rwkv_k_a'][0]), row(W['rwkv_r_k'][0]),
                      row(W['rwkv_lnx_w'][0]), row(W['rwkv_lnx_b'][0]), _state_to_z(wkv_in), b, tp, chunk,
                      d // LANES)
    if tp != t:
        o = o.reshape(b, tp, d)[:, :t].reshape(nt, d)
    wkv_out = _z_to_state(zf)
    h = matmul_res_call(o, W['rwkv_w_o'], h, tm)
    tf = _ffn_tile(W['ffn_w_gate'][0].shape[1])
    h = ffn_call(h, row(W['norm_ffn'][0]), W['ffn_w_gate'][0], W['ffn_w_up'][0], W['ffn_w_down'][0], tm, tf)
    h = ple_call(h, p[0].reshape(nt, -1), row(W['norm_ple'][0]), W['ple_w_gate'][0], W['ple_w_proj'][0], tm)

    slopes = jnp.exp2(-8.0 * jnp.arange(1, N_HEADS + 1, dtype=F32) / N_HEADS)
    if cache is None:
        kt, vt, q, kb, vtb, kmean = kvq_call(
            h, row(W['norm_kv']), row(W['norm_mix'][1]), W['kv_w_k'], W['kv_w_vT'], W['attn_w_q'],
            MOBA_BLOCK, wkt=W['kv_w_kT'], seq=t)
        attn = moba_prompt_call(slopes, q, kb, vtb, kmean.reshape(b, t // MOBA_BLOCK, d), b, t)
        unT = lambda z: jnp.transpose(z.reshape(b, N_HEADS, HEAD_DIM, t), (0, 3, 1, 2))
        k_new, v_new = unT(kt), unT(vt)
    else:
        cache_k, cache_v, page_table = cache
        k_new, v_new, q = kvq_call(h, row(W['norm_kv']), row(W['norm_mix'][1]),
                                   W['kv_w_k'], W['kv_w_v'], W['attn_w_q'], tm)

        def per_head(z):
            z4 = jnp.swapaxes(z.reshape(b, t, N_HEADS, HEAD_DIM), 1, 2)
            return jnp.pad(z4, ((0, 0), (0, 0), (0, 8 - t), (0, 0)))

        q4 = per_head(q)
        pages_t = lambda c: jnp.transpose(c, (0, 2, 3, 1))
        scores, gate = sample_scores_call(page_table, q4, pages_t(cache_k))
        attn4 = sample_attend_call(page_table, slopes, scores, gate, q4, per_head(k_new), per_head(v_new),
                                   pages_t(cache_v))
        attn = jnp.swapaxes(attn4[:, :, :t], 1, 2).reshape(nt, d)
    h = matmul_res_call(attn, W['attn_w_o'], h, tm)
    h = ffn_call(h, row(W['norm_ffn'][1]), W['ffn_w_gate'][1], W['ffn_w_up'][1], W['ffn_w_down'][1], tm, tf)
    y = ple_call(h, p[1].reshape(nt, -1), row(W['norm_ple'][1]), W['ple_w_gate'][1], W['ple_w_proj'][1], tm,
                 g_final=row(W['norm_final']))
    shp = (b, t, N_HEADS, HEAD_DIM)
    return (y.reshape(b, t, d), wkv_out[None], shift_out[None], k_new.reshape(shp), v_new.reshape(shp))


def kernel(x_prompt, x_sample, p_prompt, p_sample, state_wkv, state_shift, cache_k, cache_v, page_table,
           norm_mix, norm_ffn, norm_ple, norm_kv, norm_final, rwkv_mu, rwkv_w_rkv, rwkv_w_o, rwkv_w0,
           rwkv_w1, rwkv_w2, rwkv_a0, rwkv_a1, rwkv_a2, rwkv_g1, rwkv_g2, rwkv_k_k, rwkv_k_a, rwkv_r_k,
           rwkv_lnx_w, rwkv_lnx_b, attn_w_q, attn_w_o, kv_w_k, kv_w_v, ffn_w_gate, ffn_w_up, ffn_w_down,
           ple_w_proj, ple_w_gate):
    assert norm_mix.shape[0] == 2 and state_wkv.shape[0] == 1, "one RWKV layer then one MoBA layer"
    bf = lambda w: w.astype(BF16)
    W = dict(norm_mix=norm_mix, norm_ffn=norm_ffn, norm_ple=norm_ple, norm_kv=norm_kv, norm_final=norm_final,
             rwkv_mu=rwkv_mu, w_r=bf(rwkv_w_rkv[0, 0]), w_k=bf(rwkv_w_rkv[0, 1]), w_v=bf(rwkv_w_rkv[0, 2]),
             rwkv_w_o=bf(rwkv_w_o[0]), rwkv_w0=rwkv_w0, rwkv_w1=bf(rwkv_w1[0]), rwkv_w2=bf(rwkv_w2[0]),
             rwkv_a0=rwkv_a0, rwkv_a1=bf(rwkv_a1[0]), rwkv_a2=bf(rwkv_a2[0]), rwkv_g1=bf(rwkv_g1[0]),
             rwkv_g2=bf(rwkv_g2[0]), rwkv_k_k=rwkv_k_k, rwkv_k_a=rwkv_k_a, rwkv_r_k=rwkv_r_k,
             rwkv_lnx_w=rwkv_lnx_w, rwkv_lnx_b=rwkv_lnx_b, attn_w_q=bf(attn_w_q[0]), attn_w_o=bf(attn_w_o[0]),
             kv_w_k=bf(kv_w_k), kv_w_v=bf(kv_w_v), kv_w_kT=bf(kv_w_k.T), kv_w_vT=bf(kv_w_v.T),
             ffn_w_gate=bf(ffn_w_gate), ffn_w_up=bf(ffn_w_up), ffn_w_down=bf(ffn_w_down),
             ple_w_proj=bf(ple_w_proj), ple_w_gate=bf(ple_w_gate))
    bp = x_prompt.shape[0]
    d = x_prompt.shape[2]
    wkv0 = jnp.zeros((bp, N_HEADS, HEAD_DIM, HEAD_DIM), F32)
    shift0 = jnp.zeros((bp, d), F32)
    y_p, wkv_p, shift_p, k_p, v_p = _trunk(x_prompt, p_prompt, wkv0, shift0, W)
    y_s, wkv_s, shift_s, k_s, v_s = _trunk(x_sample, p_sample, state_wkv[0], state_shift[0], W,
                                           cache=(cache_k, cache_v, page_table))
    return (y_p, y_s, wkv_p, shift_p, k_p, v_p, wkv_s, shift_s, k_s, v_s)
```

```python
import functools
import math

import jax
import jax.numpy as jnp
from jax import lax
from jax.experimental import pallas as pl
from jax.experimental.pallas import tpu as pltpu

F32, BF16 = jnp.float32, jnp.bfloat16
N_HEADS = 16
HEAD_DIM = 64
LANES = 128
MOBA_BLOCK = 256
MOBA_TOPK = 3
PAGE_SIZE = 128
RMS_EPS = 1e-6
LNX_EPS = 64e-5
NEG = -1e30
VMEM_LIMIT_BYTES = 56 * 1024 * 1024
SCAN_CHUNK = 64
POS_LANES = 3
VT_ROWS = 80


def _cparams(*sem):
    return pltpu.CompilerParams(dimension_semantics=sem, vmem_limit_bytes=VMEM_LIMIT_BYTES)


def _rms(x, g):
    return x * lax.rsqrt(jnp.mean(x * x, axis=-1, keepdims=True) + RMS_EPS) * g


def _sigmoid(x):
    return 1.0 / (1.0 + jnp.exp(-x))


_NN = (((1,), (0,)), ((), ()))
_NT = (((1,), (1,)), ((), ()))


def _dot(a, b, dims=_NN):
    return lax.dot_general(a, b, dims, preferred_element_type=F32)


def _bf(x):
    return x.astype(BF16)


def _bdot(a, w):
    return _dot(a.astype(BF16), w)


def _split2(x):
    hi = x.astype(BF16)
    lo = (x - hi.astype(F32)).astype(BF16)
    return hi, lo


def _dot3(a, b, dims=_NN):
    ah, al = _split2(a)
    bh, bl = _split2(b)
    return _dot(ah, bh, dims) + (_dot(ah, bl, dims) + _dot(al, bh, dims))


def _full(shape):
    n = len(shape)
    return pl.BlockSpec(shape, lambda *_: (0,) * n)


def _rms_kernel(x_ref, g_ref, o_ref):
    o_ref[...] = _rms(x_ref[...], g_ref[...])


def rmsnorm_call(x, g, tm):
    t, d = x.shape
    row = pl.BlockSpec((tm, d), lambda i: (i, 0))
    return pl.pallas_call(
        _rms_kernel, grid=(t // tm,), in_specs=[row, _full((1, d))], out_specs=row,
        out_shape=jax.ShapeDtypeStruct((t, d), F32), compiler_params=_cparams("parallel"),
        name="rmsnorm")(x, g)


def _rwkv_proj_kernel(hn_ref, xp_ref, mu_ref, wr, wk, wv, w1, w2, a1, a2, g1, g2, w0, a0,
                      r_o, k_o, v_o, lw_o, a_o, g_o):
    hn = hn_ref[...]
    xx = xp_ref[...] - hn

    def mix(i):
        return hn + xx * mu_ref[i:i + 1, :]

    r_o[...] = _bdot(mix(0), wr[...])
    k_o[...] = _bdot(mix(2), wk[...])
    v_o[...] = _bdot(mix(3), wv[...])
    u = w0[...] + _bdot(jnp.tanh(_bdot(mix(1), w1[...])), w2[...])
    lw_o[...] = -math.exp(-0.5) * _sigmoid(u)
    a_o[...] = _sigmoid(a0[...] + _bdot(_bdot(mix(4), a1[...]), a2[...]))
    g_o[...] = _bdot(_sigmoid(_bdot(mix(5), g1[...])), g2[...])


def rwkv_proj_call(hn, xp, mu, wr, wk, wv, w1, w2, a1, a2, g1, g2, w0, a0, tm):
    t, d = hn.shape
    row = pl.BlockSpec((tm, d), lambda i: (i, 0))
    ws = [mu, wr, wk, wv, w1, w2, a1, a2, g1, g2, w0, a0]
    return pl.pallas_call(
        _rwkv_proj_kernel, grid=(t // tm,),
        in_specs=[row, row] + [_full(w.shape) for w in ws],
        out_specs=[row] * 6, out_shape=[jax.ShapeDtypeStruct((t, d), F32)] * 6,
        compiler_params=_cparams("parallel"), name="rwkv_proj")(hn, xp, *ws)


def _post_mixer_kernel(x_ref, wo_ref, h_ref, gf_ref, wg_ref, wu_ref, wd_ref, p_ref, gp_ref, pg_ref, pp_ref,
                       *rest, final):
    if final:
        gfin_ref, o_ref, xn_scr = rest
    else:
        o_ref, xn_scr = rest
    f = pl.program_id(1)

    @pl.when(f == 0)
    def _():
        h1 = h_ref[...] + _bdot(x_ref[...], wo_ref[...])
        xn_scr[...] = _rms(h1, gf_ref[...]).astype(BF16)
        o_ref[...] = h1

    xn = xn_scr[...]
    gt = _dot(xn, wg_ref[...])
    up = _dot(xn, wu_ref[...])
    o_ref[...] += _bdot(gt * _sigmoid(gt) * up, wd_ref[...])

    @pl.when(f == pl.num_programs(1) - 1)
    def _():
        h2 = o_ref[...]
        gate = _sigmoid(_bdot(_rms(h2, gp_ref[...]), pg_ref[...]))
        h3 = h2 + _bdot(p_ref[...], pp_ref[...]) * gate
        o_ref[...] = _rms(h3, gfin_ref[...]) if final else h3


def post_mixer_call(x, wo, h, g_ffn, wg, wu, wd, p, g_ple, ple_wg, ple_wp, tm, tf, g_final=None):
    t, d = h.shape
    f = wg.shape[1]
    row = pl.BlockSpec((tm, d), lambda i, j: (i, 0))
    const = lambda shape: pl.BlockSpec(shape, lambda i, j: (0,) * len(shape))
    ins = [x, wo, h, g_ffn, wg, wu, wd, p, g_ple, ple_wg, ple_wp]
    specs = [row, const(wo.shape), row, const((1, d)),
             pl.BlockSpec((d, tf), lambda i, j: (0, j)), pl.BlockSpec((d, tf), lambda i, j: (0, j)),
             pl.BlockSpec((tf, d), lambda i, j: (j, 0)),
             pl.BlockSpec((tm, p.shape[1]), lambda i, j: (i, 0)), const((1, d)),
             const(ple_wg.shape), const(ple_wp.shape)]
    if g_final is not None:
        ins.append(g_final)
        specs.append(const((1, d)))
    return pl.pallas_call(
        functools.partial(_post_mixer_kernel, final=g_final is not None), grid=(t // tm, f // tf),
        in_specs=specs, out_specs=row, out_shape=jax.ShapeDtypeStruct((t, d), F32),
        scratch_shapes=[pltpu.VMEM((tm, d), BF16)],
        compiler_params=_cparams("parallel", "arbitrary"), name="post_mixer")(*ins)


def _kvq_kernel(h_ref, gkv_ref, gq_ref, wk_ref, wv_ref, wq_ref, *rest, prompt):
    h = h_ref[...]
    hk = _rms(h, gkv_ref[...]).astype(BF16)
    k = _dot(hk, wk_ref[...])
    if not prompt:
        k_o, v_o, q_o = rest
        k_o[...] = k
        v_o[...] = _dot(hk, wv_ref[...])
        q_o[...] = _bdot(_rms(h, gq_ref[...]), wq_ref[...])
        return
    wkt_ref, kt_o, vt_o, q_o, kb_o, vtb_o, km_o = rest
    q_o[...] = _bdot(_rms(h, gq_ref[...]), wq_ref[...])
    kt_o[...] = _dot(wkt_ref[...], hk, _NT)
    vt = _dot(wv_ref[...], hk, _NT)
    vt_o[...] = vt
    tm, d = k.shape
    lane = lax.broadcasted_iota(jnp.int32, (tm, LANES), 1)
    rpos = lax.broadcasted_iota(jnp.int32, (tm, LANES), 0).astype(F32)
    for p in range(d // LANES):
        kp = k[:, p * LANES:(p + 1) * LANES]
        for hh in range(2):
            own = (lane < HEAD_DIM) == (hh == 0)
            pos = lane - (HEAD_DIM if hh == 0 else 0)
            extra = jnp.where(jnp.logical_and(pos >= 0, pos < POS_LANES), rpos, 0.0)
            c0 = (2 * p + hh) * LANES
            kb_o[:, c0:c0 + LANES] = jnp.where(own, kp, extra).astype(BF16)
    ones_rows = jnp.where(lax.broadcasted_iota(jnp.int32, (VT_ROWS - HEAD_DIM, tm), 0) == 0, 1.0, 0.0)
    for hd in range(d // HEAD_DIM):
        vtb_o[hd * VT_ROWS:hd * VT_ROWS + HEAD_DIM, :] = vt[hd * HEAD_DIM:(hd + 1) * HEAD_DIM, :].astype(BF16)
        vtb_o[hd * VT_ROWS + HEAD_DIM:(hd + 1) * VT_ROWS, :] = ones_rows.astype(BF16)
    km_o[...] = jnp.sum(k, axis=0, keepdims=True) * (1.0 / MOBA_BLOCK)


def kvq_call(h, gkv, gq, wk, wv, wq, tm, wkt=None, seq=None):
    t, d = h.shape
    prompt = wkt is not None
    row = pl.BlockSpec((tm, d), lambda i: (i, 0))
    ins = [h, gkv, gq, wk, wv, wq]
    specs = [row, _full(gkv.shape), _full(gq.shape), _full(wk.shape), _full(wv.shape), _full(wq.shape)]
    if prompt:
        assert tm == MOBA_BLOCK
        tps = seq // tm
        nh = d // HEAD_DIM
        tspec = lambda rows: pl.BlockSpec((None, rows, tm), lambda i: (i // tps, 0, i % tps))
        ins.append(wkt)
        specs.append(_full(wkt.shape))
        outs = [jax.ShapeDtypeStruct((t // seq, d, seq), F32)] * 2 + [
            jax.ShapeDtypeStruct((t, d), F32),
            jax.ShapeDtypeStruct((t, nh * LANES), BF16),
            jax.ShapeDtypeStruct((t // seq, nh * VT_ROWS, seq), BF16),
            jax.ShapeDtypeStruct((t // tm, 1, d), F32)]
        ospecs = [tspec(d), tspec(d), row, pl.BlockSpec((tm, nh * LANES), lambda i: (i, 0)),
                  tspec(nh * VT_ROWS), pl.BlockSpec((None, 1, d), lambda i: (i, 0, 0))]
    else:
        outs = [jax.ShapeDtypeStruct((t, d), F32)] * 3
        ospecs = [row] * 3
    return pl.pallas_call(
        functools.partial(_kvq_kernel, prompt=prompt), grid=(t // tm,), in_specs=specs,
        out_specs=ospecs, out_shape=outs, compiler_params=_cparams("parallel"), name="kvq")(*ins)


def _scan_kernel(r_ref, k_ref, v_ref, lw_ref, a_ref, g_ref, kk_ref, ka_ref, rk_ref, lnw_ref, lnb_ref,
                 z0_ref, o_ref, zout_ref, z_scr, *, C, NP, NCH):
    c = pl.program_id(2)

    @pl.when(c == 0)
    def _():
        z_scr[...] = z0_ref[...]

    lane = lax.broadcasted_iota(jnp.int32, (1, LANES), 1)
    m0 = lane < HEAD_DIM
    masks = (m0, jnp.logical_not(m0))
    pairs = range(NP)
    units = [(ch, p) for ch in range(NCH) for p in pairs]
    heads = [(u, h) for u in range(len(units)) for h in range(2)]

    def hsum(x):
        s0 = jnp.sum(jnp.where(m0, x, 0.0), axis=-1, keepdims=True)
        s1 = jnp.sum(jnp.where(m0, 0.0, x), axis=-1, keepdims=True)
        return jnp.where(m0, s0, s1)

    def tile(ref, u):
        ch, p = units[u]
        return ref[ch * C:(ch + 1) * C, p * LANES:(p + 1) * LANES]

    def ptile(ref, u):
        p = units[u][1]
        return ref[:, p * LANES:(p + 1) * LANES]

    ri = lax.broadcasted_iota(jnp.int32, (C, C), 0)
    ci = lax.broadcasted_iota(jnp.int32, (C, C), 1)
    ltri = jnp.where(ci <= ri, 1.0, 0.0).astype(BF16)
    strict_c = ci < ri
    incl_c = ci <= ri
    rw = lax.broadcasted_iota(jnp.int32, (C, 2 * C), 0)
    cw = lax.broadcasted_iota(jnp.int32, (C, 2 * C), 1)
    strict_w = jnp.where(cw < C, cw, cw - C) < rw
    right_w = cw >= C
    eye_w = cw - C == rw
    zero_c = jnp.zeros((C, LANES), BF16)
    rz = lax.broadcasted_iota(jnp.int32, (LANES, LANES), 0)
    cz = lax.broadcasted_iota(jnp.int32, (LANES, LANES), 1)
    same_head = (rz < HEAD_DIM) == (cz < HEAD_DIM)

    nu = range(len(units))

    r = [tile(r_ref, u) for u in nu]
    k = [tile(k_ref, u) for u in nu]
    v = [tile(v_ref, u) for u in nu]
    lw = [tile(lw_ref, u) for u in nu]
    a = [tile(a_ref, u) for u in nu]

    def cumsum(x):
        l1 = x.astype(BF16)
        rem = x - l1.astype(F32)
        l2 = rem.astype(BF16)
        l3 = (rem - l2.astype(F32)).astype(BF16)
        return _dot(ltri, l1) + (_dot(ltri, l2) + _dot(ltri, l3))

    cl = [cumsum(lw[u]) for u in nu]

    kmod, ars, rhs_cols, bk_t, gc_col, vs = [], [], [], [], [], []
    for u in nu:
        kkr = k[u] * ptile(kk_ref, u)
        kk = kkr / jnp.maximum(jnp.sqrt(hsum(kkr * kkr)), 1e-12)
        beta = kk * a[u]
        km = k[u] * (1.0 + (a[u] - 1.0) * ptile(ka_ref, u))
        kmod.append(km)
        igam = jnp.exp(-cl[u])
        a_t = -kk * jnp.exp(cl[u] - lw[u])
        r_t = r[u] * jnp.exp(cl[u])
        b_h = beta * igam
        k_h = km * igam
        gout = jnp.exp(cl[u][C - 1:C, :] - cl[u])
        ars.append(_bf(jnp.concatenate([a_t, r_t], axis=0)))
        rhs_cols.append(_bf(jnp.concatenate([b_h, b_h, k_h], axis=0)))
        bk_t.append(_bf(jnp.concatenate([(beta * gout).T, (km * gout).T], axis=1)))
        gc_col.append(jnp.exp(jnp.sum(lw[u].T, axis=1, keepdims=True)))
        vs.append(_bf(v[u]))

    gfull = [_dot(jnp.where(masks[h], ars[u], jnp.zeros_like(ars[u])), rhs_cols[u], _NT)
             for u, h in heads]
    w = [jnp.where(strict_w, g[:C, :2 * C], 0.0) for g in gfull]
    gk = [_bf(jnp.concatenate([jnp.where(strict_c, g[:C, 2 * C:], 0.0),
                               jnp.where(incl_c, g[C:, 2 * C:], 0.0)], axis=0)) for g in gfull]
    arb = [_bf(jnp.where(incl_c, g[C:, :C], 0.0)) for g in gfull]
    av = [_dot(gk[i], vs[u]) for i, (u, h) in enumerate(heads)]
    avs = [jnp.where(m0, av[2 * u], av[2 * u + 1]) for u in nu]

    for _ in range(int(math.log2(C))):
        wb = [_bf(x) for x in w]
        w = [jnp.where(right_w, w[i], 0.0) + _dot(wb[i][:, :C], wb[i]) for i in range(len(heads))]
    minv = [_bf(jnp.where(right_w, x, 0.0) + jnp.where(eye_w, 1.0, 0.0)) for x in w]

    z = [z_scr[p] for p in pairs]
    inv_n = 1.0 / HEAD_DIM
    for ch in range(NCH):
        us = [ch * NP + p for p in pairs]
        xs = [_dot(ars[u], _bf(z[p])) for p, u in zip(pairs, us)]
        rhs = [jnp.concatenate([zero_c, _bf(x[:C] + avs[u][:C])], axis=0) for x, u in zip(xs, us)]
        uh = [_dot(minv[2 * u + h], rhs[p]) for p, u in zip(pairs, us) for h in range(2)]
        ub = [_bf(jnp.where(m0, uh[2 * p], uh[2 * p + 1])) for p in pairs]
        oh = [_dot(arb[2 * u + h], ub[p]) for p, u in zip(pairs, us) for h in range(2)]
        zd = [_dot(bk_t[u], jnp.concatenate([ub[p], vs[u]], axis=0)) for p, u in zip(pairs, us)]
        z = [jnp.where(same_head, z[p] * gc_col[u] + zd[p], 0.0) for p, u in zip(pairs, us)]
        for p, u in zip(pairs, us):
            o = xs[p][C:] + avs[u][C:] + jnp.where(m0, oh[2 * p], oh[2 * p + 1])
            mean = hsum(o) * inv_n
            dlt = o - mean
            var = hsum(dlt * dlt) * inv_n
            on = dlt * lax.rsqrt(var + LNX_EPS) * ptile(lnw_ref, u) + ptile(lnb_ref, u)
            bonus = hsum(r[u] * kmod[u] * ptile(rk_ref, u)) * v[u]
            o_ref[ch * C:(ch + 1) * C, p * LANES:(p + 1) * LANES] = (on + bonus) * tile(g_ref, u)

    for p in pairs:
        z_scr[p] = z[p]

        @pl.when(c == pl.num_programs(2) - 1)
        def _():
            zout_ref[p] = z[p]


def scan_call(r, k, v, lw, a, g, k_k, k_a, r_k, lnw, lnb, z0, nb, seq, C, NP, NCH):
    t, d = r.shape
    npairs = d // LANES
    rows = C * NCH
    nc = seq // rows
    w = NP * LANES
    row = pl.BlockSpec((rows, w), lambda b, p, c: (b * nc + c, p))
    par = pl.BlockSpec((1, w), lambda b, p, c: (0, p))
    zspec = pl.BlockSpec((None, NP, LANES, LANES), lambda b, p, c: (b, p, 0, 0))
    return pl.pallas_call(
        functools.partial(_scan_kernel, C=C, NP=NP, NCH=NCH), grid=(nb, npairs // NP, nc),
        in_specs=[row] * 6 + [par] * 5 + [zspec],
        out_specs=[row, zspec],
        out_shape=[jax.ShapeDtypeStruct((t, d), F32),
                   jax.ShapeDtypeStruct((nb, npairs, LANES, LANES), F32)],
        scratch_shapes=[pltpu.VMEM((NP, LANES, LANES), F32)],
        compiler_params=_cparams("parallel", "parallel", "arbitrary"), name="rwkv_scan")(
            r, k, v, lw, a, g, k_k, k_a, r_k, lnw, lnb, z0)


def _state_to_z(s):
    b, h = s.shape[:2]
    st = jnp.swapaxes(s, 2, 3).reshape(b, h // 2, 2, HEAD_DIM, 1, HEAD_DIM)
    eye = jnp.eye(2, dtype=F32).reshape(1, 1, 2, 1, 2, 1)
    return (st * eye).reshape(b, h // 2, LANES, LANES)


def _z_to_state(z):
    b, hp = z.shape[:2]
    z6 = z.reshape(b, hp, 2, HEAD_DIM, 2, HEAD_DIM)
    st = jnp.stack([z6[:, :, 0, :, 0, :], z6[:, :, 1, :, 1, :]], axis=2)
    return jnp.swapaxes(st.reshape(b, hp * 2, HEAD_DIM, HEAD_DIM), 2, 3)


def _top3_rows(gate, rowi, nrow):
    sel = jnp.zeros(gate.shape, jnp.bool_)
    for _ in range(MOBA_TOPK):
        mx = jnp.max(gate, axis=0, keepdims=True)
        idx = jnp.min(jnp.where(gate == mx, rowi, nrow), axis=0, keepdims=True)
        pick = jnp.logical_and(rowi == idx, mx > -jnp.inf)
        sel = jnp.logical_or(sel, pick)
        gate = jnp.where(pick, -jnp.inf, gate)
    return sel


def _moba_prompt_kernel(sl_ref, q_ref, k_ref, vt_ref, km_ref, o_ref, sel_scr, sa_scr, sb_scr, pa_scr, pb_scr,
                        *, nb):
    p = pl.program_id(1)
    i = pl.program_id(2)
    blk = MOBA_BLOCK
    lane = lax.broadcasted_iota(jnp.int32, (1, LANES), 1)
    m0 = lane < HEAD_DIM
    masks = (m0, jnp.logical_not(m0))
    q = q_ref[...]
    km = km_ref[...]
    rowi = lax.broadcasted_iota(jnp.int32, (nb, blk), 0)
    causal = (lax.broadcasted_iota(jnp.int32, (blk, blk), 1)
              >= lax.broadcasted_iota(jnp.int32, (blk, blk), 0))
    log2e = 1.0 / math.log(2.0)
    scale2 = log2e / math.sqrt(HEAD_DIM)

    qb, slopes = [], []
    for h in range(2):
        qh = jnp.where(masks[h], q, 0.0)
        gate = jnp.where(rowi < i, _dot3(km, qh, _NT), -jnp.inf)
        sel_scr[h] = jnp.where(_top3_rows(gate, rowi, nb), 0.0, NEG)
        slope = sl_ref[2 * p + h] * log2e
        slopes.append(slope)
        pos = lane - (HEAD_DIM if h == 0 else 0)
        rest = jnp.full((1, LANES), slope, F32)
        terms = jnp.zeros((1, LANES), F32)
        for t in range(POS_LANES):
            part = rest.astype(BF16).astype(F32)
            terms = jnp.where(pos == t, part, terms)
            rest = rest - part
        qb.append(jnp.where(masks[h], q * scale2, terms).astype(BF16))

    def block_off(jb):
        return pl.multiple_of(jnp.minimum(jb, nb - 1) * blk, blk)

    def scores_into(dst, jb):
        kt = k_ref[pl.ds(block_off(jb), blk), :]
        for h in range(2):
            dst[h] = _dot(kt[:, h * LANES:(h + 1) * LANES], qb[h], _NT)

    def pv_of(p_src, jb):
        vt = vt_ref[:, pl.ds(block_off(jb), blk)]
        return [_dot(vt[h * VT_ROWS:(h + 1) * VT_ROWS, :], p_src[h]) for h in range(2)]

    def step(s_src, p_dst, rowvec, pend, carry, own):
        pv = pv_of(*pend)
        new = []
        for h in range(2):
            m, a_prev, acc = carry[h]
            s = s_src[h]
            if own:
                s = jnp.where(causal, s, NEG)
            m_new = jnp.maximum(m, jnp.max(s, axis=0, keepdims=True) + rowvec[h])
            p_dst[h] = jnp.exp2(s - (m_new - rowvec[h])).astype(BF16)
            new.append((m_new, jnp.exp2(m - m_new), a_prev * acc + pv[h]))
        return tuple(new)

    def past_rowvec(jb):
        gap = ((i - jb) * blk).astype(F32)
        jc = jnp.minimum(jb, nb - 1)
        return [jnp.where(jb < i, sel_scr[h, pl.ds(jc, 1), :] - slopes[h] * gap, NEG) for h in range(2)]

    scores_into(sb_scr, i)
    scores_into(sa_scr, 0)
    pa_scr[...] = jnp.zeros(pa_scr.shape, BF16)
    zero_row = jnp.zeros((1, blk), F32)
    init = tuple((jnp.full((1, blk), NEG, F32), zero_row, jnp.zeros((VT_ROWS, blk), F32)) for _ in range(2))
    carry = step(sb_scr, pb_scr, (zero_row, zero_row), (pa_scr, 0), init, True)

    def body(jj, carry):
        j0 = 2 * jj
        scores_into(sb_scr, j0 + 1)
        carry = step(sa_scr, pa_scr, past_rowvec(j0), (pb_scr, jnp.where(jj == 0, i, j0 - 1)), carry, False)
        scores_into(sa_scr, j0 + 2)
        return step(sb_scr, pb_scr, past_rowvec(j0 + 1), (pa_scr, j0), carry, False)

    trips = (i + 1) // 2
    carry = lax.fori_loop(0, trips, body, carry)
    pv = pv_of(pb_scr, jnp.where(trips == 0, i, 2 * trips - 1))
    acc = [carry[h][1] * carry[h][2] + pv[h] for h in range(2)]
    out_t = jnp.concatenate([acc[h][:HEAD_DIM] / acc[h][HEAD_DIM:HEAD_DIM + 1] for h in range(2)], axis=0)
    o_ref[...] = out_t.T


def moba_prompt_call(slopes, q, kb, vt, kmean, nbatch, seq):
    t, d = q.shape
    nb = seq // MOBA_BLOCK
    npairs = d // LANES
    qspec = pl.BlockSpec((MOBA_BLOCK, LANES), lambda b, p, i: (b * nb + i, p))
    tile = pltpu.VMEM((2, MOBA_BLOCK, MOBA_BLOCK), F32)
    ptile = pltpu.VMEM((2, MOBA_BLOCK, MOBA_BLOCK), BF16)
    return pl.pallas_call(
        functools.partial(_moba_prompt_kernel, nb=nb), grid=(nbatch, npairs, nb),
        in_specs=[pl.BlockSpec(memory_space=pltpu.SMEM), qspec,
                  pl.BlockSpec((seq, 2 * LANES), lambda b, p, i: (b, p)),
                  pl.BlockSpec((None, 2 * VT_ROWS, seq), lambda b, p, i: (b, p, 0)),
                  pl.BlockSpec((None, nb, LANES), lambda b, p, i: (b, 0, p))],
        out_specs=qspec, out_shape=jax.ShapeDtypeStruct((t, d), F32),
        scratch_shapes=[pltpu.VMEM((2, nb, MOBA_BLOCK), F32), tile, tile, ptile, ptile],
        compiler_params=_cparams("parallel", "parallel", "arbitrary"), name="moba_prompt")(
            slopes, q, kb, vt, kmean)


def _head_tiles(page_refs, h):
    return jnp.concatenate([ref[h] for ref in page_refs], axis=1).astype(BF16)


def _sample_scores_kernel(pt_ref, q_ref, *refs, bps):
    del pt_ref
    page_refs, (sc_ref, g_ref) = refs[:2 * bps], refs[2 * bps:]
    j = pl.program_id(1)
    rows = q_ref.shape[1]
    lane = lax.broadcasted_iota(jnp.int32, g_ref.shape[1:], 1)

    @pl.when(j == 0)
    def _():
        g_ref[...] = jnp.zeros(g_ref.shape, F32)

    for h in range(N_HEADS):
        q = q_ref[h]
        q1 = q.astype(BF16)
        r1 = q - q1.astype(F32)
        q2 = r1.astype(BF16)
        q3 = (r1 - q2.astype(F32)).astype(BF16)
        s3 = _dot(jnp.concatenate([q1, q2, q3], axis=0), _head_tiles(page_refs, h))
        sc_ref[h] = s3[:rows]
        full = s3[:rows] + (s3[rows:2 * rows] + s3[2 * rows:])
        g = g_ref[h]
        for s in range(bps):
            gcol = jnp.sum(full[:, s * MOBA_BLOCK:(s + 1) * MOBA_BLOCK], axis=1, keepdims=True)
            g = jnp.where(lane == j * bps + s, gcol * (1.0 / MOBA_BLOCK), g)
        g_ref[h] = g


def _paged_specs(page_table, page_shape, bps):
    npg = page_table.shape[1]
    ppb = MOBA_BLOCK // PAGE_SIZE
    assert ppb == 2
    return [pl.BlockSpec((None,) + page_shape,
                         functools.partial(lambda b, j, pt, o: (pt[b * npg + ppb * bps * j + o], 0, 0, 0), o=o))
            for o in range(ppb * bps)]


def _blocks_per_step(nbp):
    return max(n for n in (4, 2, 1) if nbp % n == 0)


def sample_scores_call(page_table, q4, cache_kt):
    bs, nh, rows, dh = q4.shape
    nbp = page_table.shape[1] * PAGE_SIZE // MOBA_BLOCK
    bps = _blocks_per_step(nbp)
    per_b = lambda last: pl.BlockSpec((None, nh, rows, last), lambda b, j, pt: (b, 0, 0, 0))
    gs = pltpu.PrefetchScalarGridSpec(
        num_scalar_prefetch=1, grid=(bs, nbp // bps),
        in_specs=[per_b(dh)] + _paged_specs(page_table, (nh, dh, PAGE_SIZE), bps),
        out_specs=[pl.BlockSpec((None, nh, rows, bps * MOBA_BLOCK), lambda b, j, pt: (b, 0, 0, j)), per_b(nbp)])
    return pl.pallas_call(
        functools.partial(_sample_scores_kernel, bps=bps), grid_spec=gs,
        out_shape=[jax.ShapeDtypeStruct((bs, nh, rows, nbp * MOBA_BLOCK), F32),
                   jax.ShapeDtypeStruct((bs, nh, rows, nbp), F32)],
        compiler_params=_cparams("parallel", "arbitrary"), name="sample_scores")(
            page_table.reshape(-1), q4, *([cache_kt] * (2 * bps)))


def _sample_attend_kernel(pt_ref, sl_ref, sc_ref, g_ref, q_ref, ko_ref, vo_ref, *refs, nbp, bps, past_len):
    del pt_ref
    page_refs, (o_ref, m_scr, l_scr, acc_scr, sel_scr) = refs[:2 * bps], refs[2 * bps:]
    j = pl.program_id(1)
    rows = q_ref.shape[1]
    scale = 1.0 / math.sqrt(HEAD_DIM)
    tq = lax.broadcasted_iota(jnp.int32, (rows, 1), 0)
    heads = range(N_HEADS)

    @pl.when(j == 0)
    def _():
        lane = lax.broadcasted_iota(jnp.int32, (rows, nbp), 1)
        tk = lax.broadcasted_iota(jnp.int32, (rows, rows), 1)
        for h in heads:
            gate = g_ref[h]
            sel = jnp.zeros(gate.shape, jnp.bool_)
            for _ in range(MOBA_TOPK):
                mx = jnp.max(gate, axis=1, keepdims=True)
                idx = jnp.min(jnp.where(gate == mx, lane, nbp), axis=1, keepdims=True)
                pick = jnp.logical_and(lane == idx, mx > -jnp.inf)
                sel = jnp.logical_or(sel, pick)
                gate = jnp.where(pick, -jnp.inf, gate)
            sel_scr[h] = jnp.where(sel, 0.0, NEG)
            s = _dot(q_ref[h].astype(BF16), ko_ref[h].astype(BF16), _NT) * scale
            s = jnp.where(tk <= tq, s - sl_ref[h] * (tq - tk).astype(F32), NEG)
            m = jnp.max(s, axis=1, keepdims=True)
            pr = jnp.exp(s - m)
            m_scr[h] = m
            l_scr[h] = jnp.sum(pr, axis=1, keepdims=True)
            acc_scr[h] = _dot(pr.astype(BF16), vo_ref[h].astype(BF16))

    width = bps * MOBA_BLOCK
    lane_k = lax.broadcasted_iota(jnp.int32, (rows, width), 1)
    lane_b = lax.broadcasted_iota(jnp.int32, (rows, nbp), 1)
    dist = ((past_len + tq) - (j * width + lane_k)).astype(F32)
    prs, alphas = [], []
    for h in heads:
        sel = sel_scr[h]
        bias = jnp.sum(jnp.where(lane_b == j * bps, sel, 0.0), axis=1, keepdims=True)
        for sb in range(1, bps):
            bias_sb = jnp.sum(jnp.where(lane_b == j * bps + sb, sel, 0.0), axis=1, keepdims=True)
            bias = jnp.where(lane_k >= sb * MOBA_BLOCK, bias_sb, bias)
        s = sc_ref[h] * scale - sl_ref[h] * dist + bias
        m_old = m_scr[h]
        m_new = jnp.maximum(m_old, jnp.max(s, axis=1, keepdims=True))
        alpha = jnp.exp(m_old - m_new)
        pr = jnp.exp(s - m_new)
        m_scr[h] = m_new
        l_scr[h] = alpha * l_scr[h] + jnp.sum(pr, axis=1, keepdims=True)
        prs.append(pr.astype(BF16))
        alphas.append(alpha)
    for h in heads:
        acc_scr[h] = alphas[h] * acc_scr[h] + _dot(prs[h], _head_tiles(page_refs, h), _NT)

    @pl.when(j == pl.num_programs(1) - 1)
    def _():
        for h in heads:
            o_ref[h] = acc_scr[h] / l_scr[h]


def sample_attend_call(page_table, slopes, scores, gate, q4, k_own, v_own, cache_vt):
    bs, nh, rows, dh = q4.shape
    npg = page_table.shape[1]
    nbp = npg * PAGE_SIZE // MOBA_BLOCK
    bps = _blocks_per_step(nbp)
    per_b = lambda last: pl.BlockSpec((None, nh, rows, last), lambda b, j, pt: (b, 0, 0, 0))
    gs = pltpu.PrefetchScalarGridSpec(
        num_scalar_prefetch=1, grid=(bs, nbp // bps),
        in_specs=[pl.BlockSpec(memory_space=pltpu.SMEM),
                  pl.BlockSpec((None, nh, rows, bps * MOBA_BLOCK), lambda b, j, pt: (b, 0, 0, j)),
                  per_b(nbp), per_b(dh), per_b(dh), per_b(dh)]
                 + _paged_specs(page_table, (nh, dh, PAGE_SIZE), bps),
        out_specs=per_b(dh),
        scratch_shapes=[pltpu.VMEM((nh, rows, 1), F32), pltpu.VMEM((nh, rows, 1), F32),
                        pltpu.VMEM((nh, rows, dh), F32), pltpu.VMEM((nh, rows, nbp), F32)])
    return pl.pallas_call(
        functools.partial(_sample_attend_kernel, nbp=nbp, bps=bps, past_len=npg * PAGE_SIZE),
        grid_spec=gs, out_shape=jax.ShapeDtypeStruct((bs, nh, rows, dh), F32),
        compiler_params=_cparams("parallel", "arbitrary"), name="sample_attend")(
            page_table.reshape(-1), slopes, scores, gate, q4, k_own, v_own, *([cache_vt] * (2 * bps)))


def _row_tile(t):
    for tm in (512, 256, 128):
        if t % tm == 0:
            return tm
    raise ValueError(f"token count {t} is not a multiple of 128")


def _ffn_tile(f):
    for nf in (1, 2, 4, 11, 22):
        if f % nf == 0 and (f // nf) % LANES == 0 and f // nf <= 1536:
            return f // nf
    raise ValueError(f"unsupported FFN width {f}")


def _trunk(x, p, wkv_in, shift_in, W, cache=None):
    b, t, d = x.shape
    nt = b * t
    tm = _row_tile(nt)
    h = x.reshape(nt, d)
    row = lambda v: v.reshape(1, -1)

    hn = rmsnorm_call(h, row(W['norm_mix'][0]), tm)
    hn3 = hn.reshape(b, t, d)
    xp = jnp.concatenate([shift_in[:, None, :], hn3[:, :-1]], axis=1).reshape(nt, d)
    shift_out = hn3[:, -1]
    r, k, v, lw, a, g = rwkv_proj_call(
        hn, xp, W['rwkv_mu'][0], W['w_r'], W['w_k'], W['w_v'], W['rwkv_w1'], W['rwkv_w2'],
        W['rwkv_a1'], W['rwkv_a2'], W['rwkv_g1'], W['rwkv_g2'],
        row(W['rwkv_w0'][0]), row(W['rwkv_a0'][0]), tm)
    chunk = SCAN_CHUNK if t % SCAN_CHUNK == 0 else 8
    tp = -(-t // chunk) * chunk
    seqs = (r, k, v, lw, a, g)
    if tp != t:
        seqs = tuple(jnp.pad(s.reshape(b, t, d), ((0, 0), (0, tp - t), (0, 0))).reshape(b * tp, d)
                     for s in seqs)
    o, zf = scan_call(*seqs, row(W['rwkv_k_k'][0]), row(W['rwkv_k_a'][0]), row(W['rwkv_r_k'][0]),
                      row(W['rwkv_lnx_w'][0]), row(W['rwkv_lnx_b'][0]), _state_to_z(wkv_in), b, tp, chunk,
                      d // LANES, 2 if tp % (2 * chunk) == 0 else 1)
    if tp != t:
        o = o.reshape(b, tp, d)[:, :t].reshape(nt, d)
    wkv_out = _z_to_state(zf)
    tf = _ffn_tile(W['ffn_w_gate'][0].shape[1])

    def post_mixer(x_mix, w_o, h, i, g_final=None):
        return post_mixer_call(x_mix, w_o, h, row(W['norm_ffn'][i]), W['ffn_w_gate'][i], W['ffn_w_up'][i],
                               W['ffn_w_down'][i], p[i].reshape(nt, -1), row(W['norm_ple'][i]),
                               W['ple_w_gate'][i], W['ple_w_proj'][i], tm, tf, g_final=g_final)

    h = post_mixer(o, W['rwkv_w_o'], h, 0)

    slopes = jnp.exp2(-8.0 * jnp.arange(1, N_HEADS + 1, dtype=F32) / N_HEADS)
    if cache is None:
        kt, vt, q, kb, vtb, kmean = kvq_call(
            h, row(W['norm_kv']), row(W['norm_mix'][1]), W['kv_w_k'], W['kv_w_vT'], W['attn_w_q'],
            MOBA_BLOCK, wkt=W['kv_w_kT'], seq=t)
        attn = moba_prompt_call(slopes, q, kb, vtb, kmean.reshape(b, t // MOBA_BLOCK, d), b, t)
        unT = lambda z: jnp.transpose(z.reshape(b, N_HEADS, HEAD_DIM, t), (0, 3, 1, 2))
        k_new, v_new = unT(kt), unT(vt)
    else:
        cache_k, cache_v, page_table = cache
        k_new, v_new, q = kvq_call(h, row(W['norm_kv']), row(W['norm_mix'][1]),
                                   W['kv_w_k'], W['kv_w_v'], W['attn_w_q'], tm)

        def per_head(z):
            z4 = jnp.swapaxes(z.reshape(b, t, N_HEADS, HEAD_DIM), 1, 2)
            return jnp.pad(z4, ((0, 0), (0, 0), (0, 8 - t), (0, 0)))

        q4 = per_head(q)
        pages_t = lambda c: jnp.transpose(c, (0, 2, 3, 1))
        scores, gate = sample_scores_call(page_table, q4, pages_t(cache_k))
        attn4 = sample_attend_call(page_table, slopes, scores, gate, q4, per_head(k_new), per_head(v_new),
                                   pages_t(cache_v))
        attn = jnp.swapaxes(attn4[:, :, :t], 1, 2).reshape(nt, d)
    y = post_mixer(attn, W['attn_w_o'], h, 1, g_final=row(W['norm_final']))
    shp = (b, t, N_HEADS, HEAD_DIM)
    return (y.reshape(b, t, d), wkv_out[None], shift_out[None], k_new.reshape(shp), v_new.reshape(shp))


def kernel(x_prompt, x_sample, p_prompt, p_sample, state_wkv, state_shift, cache_k, cache_v, page_table,
           norm_mix, norm_ffn, norm_ple, norm_kv, norm_final, rwkv_mu, rwkv_w_rkv, rwkv_w_o, rwkv_w0,
           rwkv_w1, rwkv_w2, rwkv_a0, rwkv_a1, rwkv_a2, rwkv_g1, rwkv_g2, rwkv_k_k, rwkv_k_a, rwkv_r_k,
           rwkv_lnx_w, rwkv_lnx_b, attn_w_q, attn_w_o, kv_w_k, kv_w_v, ffn_w_gate, ffn_w_up, ffn_w_down,
           ple_w_proj, ple_w_gate):
    assert norm_mix.shape[0] == 2 and state_wkv.shape[0] == 1, "one RWKV layer then one MoBA layer"
    bf = lambda w: w.astype(BF16)
    W = dict(norm_mix=norm_mix, norm_ffn=norm_ffn, norm_ple=norm_ple, norm_kv=norm_kv, norm_final=norm_final,
             rwkv_mu=rwkv_mu, w_r=bf(rwkv_w_rkv[0, 0]), w_k=bf(rwkv_w_rkv[0, 1]), w_v=bf(rwkv_w_rkv[0, 2]),
             rwkv_w_o=bf(rwkv_w_o[0]), rwkv_w0=rwkv_w0, rwkv_w1=bf(rwkv_w1[0]), rwkv_w2=bf(rwkv_w2[0]),
             rwkv_a0=rwkv_a0, rwkv_a1=bf(rwkv_a1[0]), rwkv_a2=bf(rwkv_a2[0]), rwkv_g1=bf(rwkv_g1[0]),
             rwkv_g2=bf(rwkv_g2[0]), rwkv_k_k=rwkv_k_k, rwkv_k_a=rwkv_k_a, rwkv_r_k=rwkv_r_k,
             rwkv_lnx_w=rwkv_lnx_w, rwkv_lnx_b=rwkv_lnx_b, attn_w_q=bf(attn_w_q[0]), attn_w_o=bf(attn_w_o[0]),
             kv_w_k=bf(kv_w_k), kv_w_v=bf(kv_w_v), kv_w_kT=bf(kv_w_k.T), kv_w_vT=bf(kv_w_v.T),
             ffn_w_gate=bf(ffn_w_gate), ffn_w_up=bf(ffn_w_up), ffn_w_down=bf(ffn_w_down),
             ple_w_proj=bf(ple_w_proj), ple_w_gate=bf(ple_w_gate))
    bp = x_prompt.shape[0]
    d = x_prompt.shape[2]
    wkv0 = jnp.zeros((bp, N_HEADS, HEAD_DIM, HEAD_DIM), F32)
    shift0 = jnp.zeros((bp, d), F32)
    y_p, wkv_p, shift_p, k_p, v_p = _trunk(x_prompt, p_prompt, wkv0, shift0, W)
    y_s, wkv_s, shift_s, k_s, v_s = _trunk(x_sample, p_sample, state_wkv[0], state_shift[0], W,
                                           cache=(cache_k, cache_v, page_table))
    return (y_p, y_s, wkv_p, shift_p, k_p, v_p, wkv_s, shift_s, k_s, v_s)
```

```python
import functools
import math

import jax
import jax.numpy as jnp
from jax import lax
from jax.experimental import pallas as pl
from jax.experimental.pallas import tpu as pltpu

F32, BF16 = jnp.float32, jnp.bfloat16
N_HEADS = 16
HEAD_DIM = 64
LANES = 128
MOBA_BLOCK = 256
MOBA_TOPK = 3
PAGE_SIZE = 128
RMS_EPS = 1e-6
LNX_EPS = 64e-5
NEG = -1e30
VMEM_LIMIT_BYTES = 56 * 1024 * 1024
SCAN_CHUNK = 64
POS_LANES = 3
UNROLL = 4
VT_ROWS = 80


def _cparams(*sem):
    return pltpu.CompilerParams(dimension_semantics=sem, vmem_limit_bytes=VMEM_LIMIT_BYTES)


def _rms(x, g):
    return x * lax.rsqrt(jnp.mean(x * x, axis=-1, keepdims=True) + RMS_EPS) * g


def _sigmoid(x):
    return 1.0 / (1.0 + jnp.exp(-x))


_NN = (((1,), (0,)), ((), ()))
_NT = (((1,), (1,)), ((), ()))


def _dot(a, b, dims=_NN):
    return lax.dot_general(a, b, dims, preferred_element_type=F32)


def _bf(x):
    return x.astype(BF16)


def _bdot(a, w):
    return _dot(a.astype(BF16), w)


def _split2(x):
    hi = x.astype(BF16)
    lo = (x - hi.astype(F32)).astype(BF16)
    return hi, lo


def _dot3(a, b, dims=_NN):
    ah, al = _split2(a)
    bh, bl = _split2(b)
    return _dot(ah, bh, dims) + (_dot(ah, bl, dims) + _dot(al, bh, dims))


def _full(shape):
    n = len(shape)
    return pl.BlockSpec(shape, lambda *_: (0,) * n)


def _rms_kernel(x_ref, g_ref, o_ref):
    o_ref[...] = _rms(x_ref[...], g_ref[...])


def rmsnorm_call(x, g, tm):
    t, d = x.shape
    row = pl.BlockSpec((tm, d), lambda i: (i, 0))
    return pl.pallas_call(
        _rms_kernel, grid=(t // tm,), in_specs=[row, _full((1, d))], out_specs=row,
        out_shape=jax.ShapeDtypeStruct((t, d), F32), compiler_params=_cparams("parallel"),
        name="rmsnorm")(x, g)


def _rwkv_proj_kernel(x_ref, xp_ref, mu_ref, wr, wk, wv, w1, w2, a1, a2, g1, g2, w0, a0, *rest, tiles_per_seq):
    if tiles_per_seq:
        gn_ref, sh_ref, r_o, k_o, v_o, lw_o, a_o, g_o, last_o = rest
        hn = _rms(x_ref[...], gn_ref[...])
        last_o[...] = hn[hn.shape[0] - 8:, :]
        prev_last = _rms(xp_ref[...], gn_ref[...])[7:8, :]
        seq_start = pl.program_id(0) % tiles_per_seq == 0
        carry = jnp.where(seq_start, sh_ref[...], prev_last)
        rowi = lax.broadcasted_iota(jnp.int32, hn.shape, 0)
        xp = jnp.where(rowi == 0, carry, pltpu.roll(hn, 1, 0))
    else:
        r_o, k_o, v_o, lw_o, a_o, g_o = rest
        hn = x_ref[...]
        xp = xp_ref[...]
    xx = xp - hn

    def mix(i):
        return hn + xx * mu_ref[i:i + 1, :]

    r_o[...] = _bdot(mix(0), wr[...])
    k_o[...] = _bdot(mix(2), wk[...])
    v_o[...] = _bdot(mix(3), wv[...])
    u = w0[...] + _bdot(jnp.tanh(_bdot(mix(1), w1[...])), w2[...])
    lw_o[...] = -math.exp(-0.5) * _sigmoid(u)
    a_o[...] = _sigmoid(a0[...] + _bdot(_bdot(mix(4), a1[...]), a2[...]))
    g_o[...] = _bdot(_sigmoid(_bdot(mix(5), g1[...])), g2[...])


def rwkv_proj_call(x, xp, mu, wr, wk, wv, w1, w2, a1, a2, g1, g2, w0, a0, tm, norm=None):
    t, d = x.shape
    row = pl.BlockSpec((tm, d), lambda i: (i, 0))
    ws = [mu, wr, wk, wv, w1, w2, a1, a2, g1, g2, w0, a0]
    outs = [jax.ShapeDtypeStruct((t, d), F32)] * 6
    ospecs = [row] * 6
    if norm is None:
        ins, specs, tps = [x, xp], [row, row], 0
    else:
        gain, shift = norm
        nseq = shift.shape[0]
        tps = t // tm // nseq
        ins = [x, x]
        specs = [row, pl.BlockSpec((8, d), lambda i: (jnp.maximum(i * (tm // 8) - 1, 0), 0))]
        ws = ws + [gain, shift]
        outs = outs + [jax.ShapeDtypeStruct((nseq, 8, d), F32)]
        ospecs = ospecs + [pl.BlockSpec((None, 8, d), lambda i: (i // tps, 0, 0))]
    wspecs = [_full(w.shape) for w in ws]
    if norm is not None:
        wspecs[-1] = pl.BlockSpec((None, 1, d), lambda i: (i // tps, 0, 0))
    return pl.pallas_call(
        functools.partial(_rwkv_proj_kernel, tiles_per_seq=tps), grid=(t // tm,),
        in_specs=specs + wspecs, out_specs=ospecs, out_shape=outs,
        compiler_params=_cparams("arbitrary"), name="rwkv_proj")(*ins, *ws)


def _post_mixer_kernel(x_ref, wo_ref, h_ref, gf_ref, wg_ref, wu_ref, wd_ref, p_ref, gp_ref, pg_ref, pp_ref,
                       *rest, final):
    if final:
        gfin_ref, o_ref, xn_scr = rest
    else:
        o_ref, xn_scr = rest
    f = pl.program_id(1)

    @pl.when(f == 0)
    def _():
        h1 = h_ref[...] + _bdot(x_ref[...], wo_ref[...])
        xn_scr[...] = _rms(h1, gf_ref[...]).astype(BF16)
        o_ref[...] = h1

    xn = xn_scr[...]
    gt = _dot(xn, wg_ref[...])
    up = _dot(xn, wu_ref[...])
    o_ref[...] += _bdot(gt * _sigmoid(gt) * up, wd_ref[...])

    @pl.when(f == pl.num_programs(1) - 1)
    def _():
        h2 = o_ref[...]
        gate = _sigmoid(_bdot(_rms(h2, gp_ref[...]), pg_ref[...]))
        h3 = h2 + _bdot(p_ref[...], pp_ref[...]) * gate
        o_ref[...] = _rms(h3, gfin_ref[...]) if final else h3


def post_mixer_call(x, wo, h, g_ffn, wg, wu, wd, p, g_ple, ple_wg, ple_wp, tm, tf, g_final=None):
    t, d = h.shape
    f = wg.shape[1]
    row = pl.BlockSpec((tm, d), lambda i, j: (i, 0))
    const = lambda shape: pl.BlockSpec(shape, lambda i, j: (0,) * len(shape))
    ins = [x, wo, h, g_ffn, wg, wu, wd, p, g_ple, ple_wg, ple_wp]
    specs = [row, const(wo.shape), row, const((1, d)),
             pl.BlockSpec((d, tf), lambda i, j: (0, j)), pl.BlockSpec((d, tf), lambda i, j: (0, j)),
             pl.BlockSpec((tf, d), lambda i, j: (j, 0)),
             pl.BlockSpec((tm, p.shape[1]), lambda i, j: (i, 0)), const((1, d)),
             const(ple_wg.shape), const(ple_wp.shape)]
    if g_final is not None:
        ins.append(g_final)
        specs.append(const((1, d)))
    return pl.pallas_call(
        functools.partial(_post_mixer_kernel, final=g_final is not None), grid=(t // tm, f // tf),
        in_specs=specs, out_specs=row, out_shape=jax.ShapeDtypeStruct((t, d), F32),
        scratch_shapes=[pltpu.VMEM((tm, d), BF16)],
        compiler_params=_cparams("parallel", "arbitrary"), name="post_mixer")(*ins)


def _kvq_kernel(h_ref, gkv_ref, gq_ref, wk_ref, wv_ref, wq_ref, *rest, prompt):
    h = h_ref[...]
    hk = _rms(h, gkv_ref[...]).astype(BF16)
    k = _dot(hk, wk_ref[...])
    if not prompt:
        k_o, v_o, q_o = rest
        k_o[...] = k
        v_o[...] = _dot(hk, wv_ref[...])
        q_o[...] = _bdot(_rms(h, gq_ref[...]), wq_ref[...])
        return
    wkt_ref, kt_o, vt_o, q_o, kb_o, vtb_o, km_o = rest
    q_o[...] = _bdot(_rms(h, gq_ref[...]), wq_ref[...])
    kt_o[...] = _dot(wkt_ref[...], hk, _NT)
    vt = _dot(wv_ref[...], hk, _NT)
    vt_o[...] = vt
    tm, d = k.shape
    lane = lax.broadcasted_iota(jnp.int32, (tm, LANES), 1)
    rpos = lax.broadcasted_iota(jnp.int32, (tm, LANES), 0).astype(F32)
    for p in range(d // LANES):
        kp = k[:, p * LANES:(p + 1) * LANES]
        for hh in range(2):
            own = (lane < HEAD_DIM) == (hh == 0)
            pos = lane - (HEAD_DIM if hh == 0 else 0)
            extra = jnp.where(jnp.logical_and(pos >= 0, pos < POS_LANES), rpos, 0.0)
            c0 = (2 * p + hh) * LANES
            kb_o[:, c0:c0 + LANES] = jnp.where(own, kp, extra).astype(BF16)
    ones_rows = jnp.where(lax.broadcasted_iota(jnp.int32, (VT_ROWS - HEAD_DIM, tm), 0) == 0, 1.0, 0.0)
    for hd in range(d // HEAD_DIM):
        vtb_o[hd * VT_ROWS:hd * VT_ROWS + HEAD_DIM, :] = vt[hd * HEAD_DIM:(hd + 1) * HEAD_DIM, :].astype(BF16)
        vtb_o[hd * VT_ROWS + HEAD_DIM:(hd + 1) * VT_ROWS, :] = ones_rows.astype(BF16)
    km_o[...] = jnp.sum(k, axis=0, keepdims=True) * (1.0 / MOBA_BLOCK)


def kvq_call(h, gkv, gq, wk, wv, wq, tm, wkt=None, seq=None):
    t, d = h.shape
    prompt = wkt is not None
    row = pl.BlockSpec((tm, d), lambda i: (i, 0))
    ins = [h, gkv, gq, wk, wv, wq]
    specs = [row, _full(gkv.shape), _full(gq.shape), _full(wk.shape), _full(wv.shape), _full(wq.shape)]
    if prompt:
        assert tm == MOBA_BLOCK
        tps = seq // tm
        nh = d // HEAD_DIM
        tspec = lambda rows: pl.BlockSpec((None, rows, tm), lambda i: (i // tps, 0, i % tps))
        ins.append(wkt)
        specs.append(_full(wkt.shape))
        outs = [jax.ShapeDtypeStruct((t // seq, d, seq), F32)] * 2 + [
            jax.ShapeDtypeStruct((t, d), F32),
            jax.ShapeDtypeStruct((t, nh * LANES), BF16),
            jax.ShapeDtypeStruct((t // seq, nh * VT_ROWS, seq), BF16),
            jax.ShapeDtypeStruct((t // tm, 1, d), F32)]
        ospecs = [tspec(d), tspec(d), row, pl.BlockSpec((tm, nh * LANES), lambda i: (i, 0)),
                  tspec(nh * VT_ROWS), pl.BlockSpec((None, 1, d), lambda i: (i, 0, 0))]
    else:
        outs = [jax.ShapeDtypeStruct((t, d), F32)] * 3
        ospecs = [row] * 3
    return pl.pallas_call(
        functools.partial(_kvq_kernel, prompt=prompt), grid=(t // tm,), in_specs=specs,
        out_specs=ospecs, out_shape=outs, compiler_params=_cparams("parallel"), name="kvq")(*ins)


def _scan_kernel(r_ref, k_ref, v_ref, lw_ref, a_ref, g_ref, kk_ref, ka_ref, rk_ref, lnw_ref, lnb_ref,
                 z0_ref, o_ref, zout_ref, z_scr, *, C, NP, NCH):
    c = pl.program_id(2)

    @pl.when(c == 0)
    def _():
        z_scr[...] = z0_ref[...]

    lane = lax.broadcasted_iota(jnp.int32, (1, LANES), 1)
    m0 = lane < HEAD_DIM
    masks = (m0, jnp.logical_not(m0))
    pairs = range(NP)
    units = [(ch, p) for ch in range(NCH) for p in pairs]
    heads = [(u, h) for u in range(len(units)) for h in range(2)]

    def hsum(x):
        s0 = jnp.sum(jnp.where(m0, x, 0.0), axis=-1, keepdims=True)
        s1 = jnp.sum(jnp.where(m0, 0.0, x), axis=-1, keepdims=True)
        return jnp.where(m0, s0, s1)

    def tile(ref, u):
        ch, p = units[u]
        return ref[ch * C:(ch + 1) * C, p * LANES:(p + 1) * LANES]

    def ptile(ref, u):
        p = units[u][1]
        return ref[:, p * LANES:(p + 1) * LANES]

    ri = lax.broadcasted_iota(jnp.int32, (C, C), 0)
    ci = lax.broadcasted_iota(jnp.int32, (C, C), 1)
    ltri = jnp.where(ci <= ri, 1.0, 0.0).astype(BF16)
    strict_c = ci < ri
    incl_c = ci <= ri
    rw = lax.broadcasted_iota(jnp.int32, (C, 2 * C), 0)
    cw = lax.broadcasted_iota(jnp.int32, (C, 2 * C), 1)
    strict_w = jnp.where(cw < C, cw, cw - C) < rw
    right_w = cw >= C
    eye_w = cw - C == rw
    zero_c = jnp.zeros((C, LANES), BF16)
    rz = lax.broadcasted_iota(jnp.int32, (LANES, LANES), 0)
    cz = lax.broadcasted_iota(jnp.int32, (LANES, LANES), 1)
    same_head = (rz < HEAD_DIM) == (cz < HEAD_DIM)

    nu = range(len(units))

    r = [tile(r_ref, u) for u in nu]
    k = [tile(k_ref, u) for u in nu]
    v = [tile(v_ref, u) for u in nu]
    lw = [tile(lw_ref, u) for u in nu]
    a = [tile(a_ref, u) for u in nu]

    def cumsum(x):
        l1 = x.astype(BF16)
        rem = x - l1.astype(F32)
        l2 = rem.astype(BF16)
        l3 = (rem - l2.astype(F32)).astype(BF16)
        return _dot(ltri, l1) + (_dot(ltri, l2) + _dot(ltri, l3))

    cl = [cumsum(lw[u]) for u in nu]

    kmod, ars, rhs_cols, bk_t, gc_col, vs = [], [], [], [], [], []
    for u in nu:
        kkr = k[u] * ptile(kk_ref, u)
        kk = kkr / jnp.maximum(jnp.sqrt(hsum(kkr * kkr)), 1e-12)
        beta = kk * a[u]
        km = k[u] * (1.0 + (a[u] - 1.0) * ptile(ka_ref, u))
        kmod.append(km)
        igam = jnp.exp(-cl[u])
        a_t = -kk * jnp.exp(cl[u] - lw[u])
        r_t = r[u] * jnp.exp(cl[u])
        b_h = beta * igam
        k_h = km * igam
        gout = jnp.exp(cl[u][C - 1:C, :] - cl[u])
        ars.append(_bf(jnp.concatenate([a_t, r_t], axis=0)))
        rhs_cols.append(_bf(jnp.concatenate([b_h, b_h, k_h], axis=0)))
        bk_t.append(_bf(jnp.concatenate([(beta * gout).T, (km * gout).T], axis=1)))
        gc_col.append(jnp.exp(jnp.sum(lw[u].T, axis=1, keepdims=True)))
        vs.append(_bf(v[u]))

    gfull = [_dot(jnp.where(masks[h], ars[u], jnp.zeros_like(ars[u])), rhs_cols[u], _NT)
             for u, h in heads]
    w = [jnp.where(strict_w, g[:C, :2 * C], 0.0) for g in gfull]
    gk = [_bf(jnp.concatenate([jnp.where(strict_c, g[:C, 2 * C:], 0.0),
                               jnp.where(incl_c, g[C:, 2 * C:], 0.0)], axis=0)) for g in gfull]
    arb = [_bf(jnp.where(incl_c, g[C:, :C], 0.0)) for g in gfull]
    av = [_dot(gk[i], vs[u]) for i, (u, h) in enumerate(heads)]
    avs = [jnp.where(m0, av[2 * u], av[2 * u + 1]) for u in nu]

    for _ in range(int(math.log2(C))):
        wb = [_bf(x) for x in w]
        w = [jnp.where(right_w, w[i], 0.0) + _dot(wb[i][:, :C], wb[i]) for i in range(len(heads))]
    minv = [_bf(jnp.where(right_w, x, 0.0) + jnp.where(eye_w, 1.0, 0.0)) for x in w]

    z = [z_scr[p] for p in pairs]
    inv_n = 1.0 / HEAD_DIM
    for ch in range(NCH):
        us = [ch * NP + p for p in pairs]
        xs = [_dot(ars[u], _bf(z[p])) for p, u in zip(pairs, us)]
        rhs = [jnp.concatenate([zero_c, _bf(x[:C] + avs[u][:C])], axis=0) for x, u in zip(xs, us)]
        uh = [_dot(minv[2 * u + h], rhs[p]) for p, u in zip(pairs, us) for h in range(2)]
        ub = [_bf(jnp.where(m0, uh[2 * p], uh[2 * p + 1])) for p in pairs]
        oh = [_dot(arb[2 * u + h], ub[p]) for p, u in zip(pairs, us) for h in range(2)]
        zd = [_dot(bk_t[u], jnp.concatenate([ub[p], vs[u]], axis=0)) for p, u in zip(pairs, us)]
        z = [jnp.where(same_head, z[p] * gc_col[u] + zd[p], 0.0) for p, u in zip(pairs, us)]
        for p, u in zip(pairs, us):
            o = xs[p][C:] + avs[u][C:] + jnp.where(m0, oh[2 * p], oh[2 * p + 1])
            mean = hsum(o) * inv_n
            dlt = o - mean
            var = hsum(dlt * dlt) * inv_n
            on = dlt * lax.rsqrt(var + LNX_EPS) * ptile(lnw_ref, u) + ptile(lnb_ref, u)
            bonus = hsum(r[u] * kmod[u] * ptile(rk_ref, u)) * v[u]
            o_ref[ch * C:(ch + 1) * C, p * LANES:(p + 1) * LANES] = (on + bonus) * tile(g_ref, u)

    for p in pairs:
        z_scr[p] = z[p]

        @pl.when(c == pl.num_programs(2) - 1)
        def _():
            zout_ref[p] = z[p]


def scan_call(r, k, v, lw, a, g, k_k, k_a, r_k, lnw, lnb, z0, nb, seq, C, NP, NCH):
    t, d = r.shape
    npairs = d // LANES
    rows = C * NCH
    nc = seq // rows
    w = NP * LANES
    row = pl.BlockSpec((rows, w), lambda b, p, c: (b * nc + c, p))
    par = pl.BlockSpec((1, w), lambda b, p, c: (0, p))
    zspec = pl.BlockSpec((None, NP, LANES, LANES), lambda b, p, c: (b, p, 0, 0))
    return pl.pallas_call(
        functools.partial(_scan_kernel, C=C, NP=NP, NCH=NCH), grid=(nb, npairs // NP, nc),
        in_specs=[row] * 6 + [par] * 5 + [zspec],
        out_specs=[row, zspec],
        out_shape=[jax.ShapeDtypeStruct((t, d), F32),
                   jax.ShapeDtypeStruct((nb, npairs, LANES, LANES), F32)],
        scratch_shapes=[pltpu.VMEM((NP, LANES, LANES), F32)],
        compiler_params=_cparams("parallel", "parallel", "arbitrary"), name="rwkv_scan")(
            r, k, v, lw, a, g, k_k, k_a, r_k, lnw, lnb, z0)


def _state_to_z(s):
    b, h = s.shape[:2]
    st = jnp.swapaxes(s, 2, 3).reshape(b, h // 2, 2, HEAD_DIM, 1, HEAD_DIM)
    eye = jnp.eye(2, dtype=F32).reshape(1, 1, 2, 1, 2, 1)
    return (st * eye).reshape(b, h // 2, LANES, LANES)


def _z_to_state(z):
    b, hp = z.shape[:2]
    z6 = z.reshape(b, hp, 2, HEAD_DIM, 2, HEAD_DIM)
    st = jnp.stack([z6[:, :, 0, :, 0, :], z6[:, :, 1, :, 1, :]], axis=2)
    return jnp.swapaxes(st.reshape(b, hp * 2, HEAD_DIM, HEAD_DIM), 2, 3)


def _top3_rows(gate, rowi, nrow):
    sel = jnp.zeros(gate.shape, jnp.bool_)
    for _ in range(MOBA_TOPK):
        mx = jnp.max(gate, axis=0, keepdims=True)
        idx = jnp.min(jnp.where(gate == mx, rowi, nrow), axis=0, keepdims=True)
        pick = jnp.logical_and(rowi == idx, mx > -jnp.inf)
        sel = jnp.logical_or(sel, pick)
        gate = jnp.where(pick, -jnp.inf, gate)
    return sel


def _moba_prompt_kernel(sl_ref, q_ref, k_ref, vt_ref, km_ref, o_ref, sel_scr, sa_scr, sb_scr, pa_scr, pb_scr,
                        *, nb):
    p = pl.program_id(1)
    i = pl.program_id(2)
    blk = MOBA_BLOCK
    lane = lax.broadcasted_iota(jnp.int32, (1, LANES), 1)
    m0 = lane < HEAD_DIM
    masks = (m0, jnp.logical_not(m0))
    q = q_ref[...]
    km = km_ref[...]
    rowi = lax.broadcasted_iota(jnp.int32, (nb, blk), 0)
    causal = (lax.broadcasted_iota(jnp.int32, (blk, blk), 1)
              >= lax.broadcasted_iota(jnp.int32, (blk, blk), 0))
    log2e = 1.0 / math.log(2.0)
    scale2 = log2e / math.sqrt(HEAD_DIM)

    qb, slopes = [], []
    for h in range(2):
        qh = jnp.where(masks[h], q, 0.0)
        gate = jnp.where(rowi < i, _dot3(km, qh, _NT), -jnp.inf)
        sel_scr[h] = jnp.where(_top3_rows(gate, rowi, nb), 0.0, NEG)
        slope = sl_ref[2 * p + h] * log2e
        slopes.append(slope)
        pos = lane - (HEAD_DIM if h == 0 else 0)
        rest = jnp.full((1, LANES), slope, F32)
        terms = jnp.zeros((1, LANES), F32)
        for t in range(POS_LANES):
            part = rest.astype(BF16).astype(F32)
            terms = jnp.where(pos == t, part, terms)
            rest = rest - part
        qb.append(jnp.where(masks[h], q * scale2, terms).astype(BF16))

    def block_off(jb):
        return pl.multiple_of(jnp.minimum(jb, nb - 1) * blk, blk)

    def scores_into(dst, jb):
        kt = k_ref[pl.ds(block_off(jb), blk), :]
        for h in range(2):
            dst[h] = _dot(kt[:, h * LANES:(h + 1) * LANES], qb[h], _NT)

    def pv_of(p_src, jb):
        vt = vt_ref[:, pl.ds(block_off(jb), blk)]
        return [_dot(vt[h * VT_ROWS:(h + 1) * VT_ROWS, :], p_src[h]) for h in range(2)]

    def step(s_src, p_dst, rowvec, pend, carry, own):
        pv = None if own else pv_of(*pend)
        new = []
        for h in range(2):
            m, a_prev, acc = carry[h]
            s = s_src[h]
            if own:
                s = jnp.where(causal, s, NEG)
            m_new = jnp.maximum(m, jnp.max(s, axis=0, keepdims=True) + rowvec[h])
            p_dst[h] = jnp.exp2(s - (m_new - rowvec[h])).astype(BF16)
            new.append((m_new, jnp.exp2(m - m_new), acc if own else a_prev * acc + pv[h]))
        return tuple(new)

    def past_rowvec(jb):
        gap = ((i - jb) * blk).astype(F32)
        jc = jnp.minimum(jb, nb - 1)
        return [jnp.where(jb < i, sel_scr[h, pl.ds(jc, 1), :] - slopes[h] * gap, NEG) for h in range(2)]

    scores_into(sb_scr, i)
    scores_into(sa_scr, 0)
    zero_row = jnp.zeros((1, blk), F32)
    init = tuple((jnp.full((1, blk), NEG, F32), zero_row, jnp.zeros((VT_ROWS, blk), F32)) for _ in range(2))
    carry = step(sb_scr, pb_scr, (zero_row, zero_row), None, init, True)

    def body(jj, carry):
        j0 = UNROLL * jj
        for u in range(0, UNROLL, 2):
            scores_into(sb_scr, j0 + u + 1)
            prev = jnp.where(jj == 0, i, j0 - 1) if u == 0 else j0 + u - 1
            carry = step(sa_scr, pa_scr, past_rowvec(j0 + u), (pb_scr, prev), carry, False)
            scores_into(sa_scr, j0 + u + 2)
            carry = step(sb_scr, pb_scr, past_rowvec(j0 + u + 1), (pa_scr, j0 + u), carry, False)
        return carry

    trips = (i + UNROLL - 1) // UNROLL
    carry = lax.fori_loop(0, trips, body, carry)
    pv = pv_of(pb_scr, jnp.where(trips == 0, i, UNROLL * trips - 1))
    acc = [carry[h][1] * carry[h][2] + pv[h] for h in range(2)]
    out_t = jnp.concatenate([acc[h][:HEAD_DIM] / acc[h][HEAD_DIM:HEAD_DIM + 1] for h in range(2)], axis=0)
    o_ref[...] = out_t.T


def moba_prompt_call(slopes, q, kb, vt, kmean, nbatch, seq):
    t, d = q.shape
    nb = seq // MOBA_BLOCK
    npairs = d // LANES
    qspec = pl.BlockSpec((MOBA_BLOCK, LANES), lambda b, p, i: (b * nb + i, p))
    tile = pltpu.VMEM((2, MOBA_BLOCK, MOBA_BLOCK), F32)
    ptile = pltpu.VMEM((2, MOBA_BLOCK, MOBA_BLOCK), BF16)
    return pl.pallas_call(
        functools.partial(_moba_prompt_kernel, nb=nb), grid=(nbatch, npairs, nb),
        in_specs=[pl.BlockSpec(memory_space=pltpu.SMEM), qspec,
                  pl.BlockSpec((seq, 2 * LANES), lambda b, p, i: (b, p)),
                  pl.BlockSpec((None, 2 * VT_ROWS, seq), lambda b, p, i: (b, p, 0)),
                  pl.BlockSpec((None, nb, LANES), lambda b, p, i: (b, 0, p))],
        out_specs=qspec, out_shape=jax.ShapeDtypeStruct((t, d), F32),
        scratch_shapes=[pltpu.VMEM((2, nb, MOBA_BLOCK), F32), tile, tile, ptile, ptile],
        compiler_params=_cparams("parallel", "parallel", "arbitrary"), name="moba_prompt")(
            slopes, q, kb, vt, kmean)


def _head_tiles(page_refs, h):
    return jnp.concatenate([ref[h] for ref in page_refs], axis=1).astype(BF16)


def _sample_scores_kernel(pt_ref, q_ref, *refs, bps):
    del pt_ref
    page_refs, (sc_ref, g_ref) = refs[:2 * bps], refs[2 * bps:]
    j = pl.program_id(1)
    rows = q_ref.shape[1]
    lane = lax.broadcasted_iota(jnp.int32, g_ref.shape[1:], 1)

    @pl.when(j == 0)
    def _():
        g_ref[...] = jnp.zeros(g_ref.shape, F32)

    for h in range(N_HEADS):
        q = q_ref[h]
        q1 = q.astype(BF16)
        r1 = q - q1.astype(F32)
        q2 = r1.astype(BF16)
        q3 = (r1 - q2.astype(F32)).astype(BF16)
        s3 = _dot(jnp.concatenate([q1, q2, q3], axis=0), _head_tiles(page_refs, h))
        sc_ref[h] = s3[:rows]
        full = s3[:rows] + (s3[rows:2 * rows] + s3[2 * rows:])
        g = g_ref[h]
        for s in range(bps):
            gcol = jnp.sum(full[:, s * MOBA_BLOCK:(s + 1) * MOBA_BLOCK], axis=1, keepdims=True)
            g = jnp.where(lane == j * bps + s, gcol * (1.0 / MOBA_BLOCK), g)
        g_ref[h] = g


def _paged_specs(page_table, page_shape, bps):
    npg = page_table.shape[1]
    ppb = MOBA_BLOCK // PAGE_SIZE
    assert ppb == 2
    return [pl.BlockSpec((None,) + page_shape,
                         functools.partial(lambda b, j, pt, o: (pt[b * npg + ppb * bps * j + o], 0, 0, 0), o=o))
            for o in range(ppb * bps)]


def _blocks_per_step(nbp):
    return max(n for n in (4, 2, 1) if nbp % n == 0)


def sample_scores_call(page_table, q4, cache_kt):
    bs, nh, rows, dh = q4.shape
    nbp = page_table.shape[1] * PAGE_SIZE // MOBA_BLOCK
    bps = _blocks_per_step(nbp)
    per_b = lambda last: pl.BlockSpec((None, nh, rows, last), lambda b, j, pt: (b, 0, 0, 0))
    gs = pltpu.PrefetchScalarGridSpec(
        num_scalar_prefetch=1, grid=(bs, nbp // bps),
        in_specs=[per_b(dh)] + _paged_specs(page_table, (nh, dh, PAGE_SIZE), bps),
        out_specs=[pl.BlockSpec((None, nh, rows, bps * MOBA_BLOCK), lambda b, j, pt: (b, 0, 0, j)), per_b(nbp)])
    return pl.pallas_call(
        functools.partial(_sample_scores_kernel, bps=bps), grid_spec=gs,
        out_shape=[jax.ShapeDtypeStruct((bs, nh, rows, nbp * MOBA_BLOCK), F32),
                   jax.ShapeDtypeStruct((bs, nh, rows, nbp), F32)],
        compiler_params=_cparams("parallel", "arbitrary"), name="sample_scores")(
            page_table.reshape(-1), q4, *([cache_kt] * (2 * bps)))


def _sample_attend_kernel(pt_ref, sl_ref, sc_ref, g_ref, q_ref, ko_ref, vo_ref, *refs, nbp, bps, past_len):
    del pt_ref
    page_refs, (o_ref, m_scr, l_scr, acc_scr, sel_scr) = refs[:2 * bps], refs[2 * bps:]
    j = pl.program_id(1)
    rows = q_ref.shape[1]
    scale = 1.0 / math.sqrt(HEAD_DIM)
    tq = lax.broadcasted_iota(jnp.int32, (rows, 1), 0)
    heads = range(N_HEADS)

    @pl.when(j == 0)
    def _():
        lane = lax.broadcasted_iota(jnp.int32, (rows, nbp), 1)
        tk = lax.broadcasted_iota(jnp.int32, (rows, rows), 1)
        for h in heads:
            gate = g_ref[h]
            sel = jnp.zeros(gate.shape, jnp.bool_)
            for _ in range(MOBA_TOPK):
                mx = jnp.max(gate, axis=1, keepdims=True)
                idx = jnp.min(jnp.where(gate == mx, lane, nbp), axis=1, keepdims=True)
                pick = jnp.logical_and(lane == idx, mx > -jnp.inf)
                sel = jnp.logical_or(sel, pick)
                gate = jnp.where(pick, -jnp.inf, gate)
            sel_scr[h] = jnp.where(sel, 0.0, NEG)
            s = _dot(q_ref[h].astype(BF16), ko_ref[h].astype(BF16), _NT) * scale
            s = jnp.where(tk <= tq, s - sl_ref[h] * (tq - tk).astype(F32), NEG)
            m = jnp.max(s, axis=1, keepdims=True)
            pr = jnp.exp(s - m)
            m_scr[h] = m
            l_scr[h] = jnp.sum(pr, axis=1, keepdims=True)
            acc_scr[h] = _dot(pr.astype(BF16), vo_ref[h].astype(BF16))

    width = bps * MOBA_BLOCK
    lane_k = lax.broadcasted_iota(jnp.int32, (rows, width), 1)
    lane_b = lax.broadcasted_iota(jnp.int32, (rows, nbp), 1)
    dist = ((past_len + tq) - (j * width + lane_k)).astype(F32)
    prs, alphas = [], []
    for h in heads:
        sel = sel_scr[h]
        bias = jnp.sum(jnp.where(lane_b == j * bps, sel, 0.0), axis=1, keepdims=True)
        for sb in range(1, bps):
            bias_sb = jnp.sum(jnp.where(lane_b == j * bps + sb, sel, 0.0), axis=1, keepdims=True)
            bias = jnp.where(lane_k >= sb * MOBA_BLOCK, bias_sb, bias)
        s = sc_ref[h] * scale - sl_ref[h] * dist + bias
        m_old = m_scr[h]
        m_new = jnp.maximum(m_old, jnp.max(s, axis=1, keepdims=True))
        alpha = jnp.exp(m_old - m_new)
        pr = jnp.exp(s - m_new)
        m_scr[h] = m_new
        l_scr[h] = alpha * l_scr[h] + jnp.sum(pr, axis=1, keepdims=True)
        prs.append(pr.astype(BF16))
        alphas.append(alpha)
    for h in heads:
        acc_scr[h] = alphas[h] * acc_scr[h] + _dot(prs[h], _head_tiles(page_refs, h), _NT)

    @pl.when(j == pl.num_programs(1) - 1)
    def _():
        for h in heads:
            o_ref[h] = acc_scr[h] / l_scr[h]


def sample_attend_call(page_table, slopes, scores, gate, q4, k_own, v_own, cache_vt):
    bs, nh, rows, dh = q4.shape
    npg = page_table.shape[1]
    nbp = npg * PAGE_SIZE // MOBA_BLOCK
    bps = _blocks_per_step(nbp)
    per_b = lambda last: pl.BlockSpec((None, nh, rows, last), lambda b, j, pt: (b, 0, 0, 0))
    gs = pltpu.PrefetchScalarGridSpec(
        num_scalar_prefetch=1, grid=(bs, nbp // bps),
        in_specs=[pl.BlockSpec(memory_space=pltpu.SMEM),
                  pl.BlockSpec((None, nh, rows, bps * MOBA_BLOCK), lambda b, j, pt: (b, 0, 0, j)),
                  per_b(nbp), per_b(dh), per_b(dh), per_b(dh)]
                 + _paged_specs(page_table, (nh, dh, PAGE_SIZE), bps),
        out_specs=per_b(dh),
        scratch_shapes=[pltpu.VMEM((nh, rows, 1), F32), pltpu.VMEM((nh, rows, 1), F32),
                        pltpu.VMEM((nh, rows, dh), F32), pltpu.VMEM((nh, rows, nbp), F32)])
    return pl.pallas_call(
        functools.partial(_sample_attend_kernel, nbp=nbp, bps=bps, past_len=npg * PAGE_SIZE),
        grid_spec=gs, out_shape=jax.ShapeDtypeStruct((bs, nh, rows, dh), F32),
        compiler_params=_cparams("parallel", "arbitrary"), name="sample_attend")(
            page_table.reshape(-1), slopes, scores, gate, q4, k_own, v_own, *([cache_vt] * (2 * bps)))


def _row_tile(t):
    for tm in (512, 256, 128):
        if t % tm == 0:
            return tm
    raise ValueError(f"token count {t} is not a multiple of 128")


def _ffn_tile(f):
    for nf in (1, 2, 4, 11, 22):
        if f % nf == 0 and (f // nf) % LANES == 0 and f // nf <= 1536:
            return f // nf
    raise ValueError(f"unsupported FFN width {f}")


def _trunk(x, p, wkv_in, shift_in, W, cache=None):
    b, t, d = x.shape
    nt = b * t
    tm = _row_tile(nt)
    h = x.reshape(nt, d)
    row = lambda v: v.reshape(1, -1)

    proj_w = (W['rwkv_mu'][0], W['w_r'], W['w_k'], W['w_v'], W['rwkv_w1'], W['rwkv_w2'],
              W['rwkv_a1'], W['rwkv_a2'], W['rwkv_g1'], W['rwkv_g2'], row(W['rwkv_w0'][0]), row(W['rwkv_a0'][0]))
    if t % tm == 0:
        r, k, v, lw, a, g, last = rwkv_proj_call(h, None, *proj_w, tm,
                                                 norm=(row(W['norm_mix'][0]), shift_in[:, None, :]))
        shift_out = last[:, -1]
    else:
        hn = rmsnorm_call(h, row(W['norm_mix'][0]), tm)
        hn3 = hn.reshape(b, t, d)
        xp = jnp.concatenate([shift_in[:, None, :], hn3[:, :-1]], axis=1).reshape(nt, d)
        shift_out = hn3[:, -1]
        r, k, v, lw, a, g = rwkv_proj_call(hn, xp, *proj_w, tm)
    chunk = SCAN_CHUNK if t % SCAN_CHUNK == 0 else 8
    tp = -(-t // chunk) * chunk
    seqs = (r, k, v, lw, a, g)
    if tp != t:
        seqs = tuple(jnp.pad(s.reshape(b, t, d), ((0, 0), (0, tp - t), (0, 0))).reshape(b * tp, d)
                     for s in seqs)
    o, zf = scan_call(*seqs, row(W['rwkv_k_k'][0]), row(W['rwkv_k_a'][0]), row(W['rwkv_r_k'][0]),
                      row(W['rwkv_lnx_w'][0]), row(W['rwkv_lnx_b'][0]), _state_to_z(wkv_in), b, tp, chunk,
                      d // LANES, 2 if tp % (2 * chunk) == 0 else 1)
    if tp != t:
        o = o.reshape(b, tp, d)[:, :t].reshape(nt, d)
    wkv_out = _z_to_state(zf)
    tf = _ffn_tile(W['ffn_w_gate'][0].shape[1])

    def post_mixer(x_mix, w_o, h, i, g_final=None):
        return post_mixer_call(x_mix, w_o, h, row(W['norm_ffn'][i]), W['ffn_w_gate'][i], W['ffn_w_up'][i],
                               W['ffn_w_down'][i], p[i].reshape(nt, -1), row(W['norm_ple'][i]),
                               W['ple_w_gate'][i], W['ple_w_proj'][i], tm, tf, g_final=g_final)

    h = post_mixer(o, W['rwkv_w_o'], h, 0)

    slopes = jnp.exp2(-8.0 * jnp.arange(1, N_HEADS + 1, dtype=F32) / N_HEADS)
    if cache is None:
        kt, vt, q, kb, vtb, kmean = kvq_call(
            h, row(W['norm_kv']), row(W['norm_mix'][1]), W['kv_w_k'], W['kv_w_vT'], W['attn_w_q'],
            MOBA_BLOCK, wkt=W['kv_w_kT'], seq=t)
        attn = moba_prompt_call(slopes, q, kb, vtb, kmean.reshape(b, t // MOBA_BLOCK, d), b, t)
        unT = lambda z: jnp.transpose(z.reshape(b, N_HEADS, HEAD_DIM, t), (0, 3, 1, 2))
        k_new, v_new = unT(kt), unT(vt)
    else:
        cache_k, cache_v, page_table = cache
        k_new, v_new, q = kvq_call(h, row(W['norm_kv']), row(W['norm_mix'][1]),
                                   W['kv_w_k'], W['kv_w_v'], W['attn_w_q'], tm)

        def per_head(z):
            z4 = jnp.swapaxes(z.reshape(b, t, N_HEADS, HEAD_DIM), 1, 2)
            return jnp.pad(z4, ((0, 0), (0, 0), (0, 8 - t), (0, 0)))

        q4 = per_head(q)
        pages_t = lambda c: jnp.transpose(c, (0, 2, 3, 1))
        scores, gate = sample_scores_call(page_table, q4, pages_t(cache_k))
        attn4 = sample_attend_call(page_table, slopes, scores, gate, q4, per_head(k_new), per_head(v_new),
                                   pages_t(cache_v))
        attn = jnp.swapaxes(attn4[:, :, :t], 1, 2).reshape(nt, d)
    y = post_mixer(attn, W['attn_w_o'], h, 1, g_final=row(W['norm_final']))
    shp = (b, t, N_HEADS, HEAD_DIM)
    return (y.reshape(b, t, d), wkv_out[None], shift_out[None], k_new.reshape(shp), v_new.reshape(shp))


def kernel(x_prompt, x_sample, p_prompt, p_sample, state_wkv, state_shift, cache_k, cache_v, page_table,
           norm_mix, norm_ffn, norm_ple, norm_kv, norm_final, rwkv_mu, rwkv_w_rkv, rwkv_w_o, rwkv_w0,
           rwkv_w1, rwkv_w2, rwkv_a0, rwkv_a1, rwkv_a2, rwkv_g1, rwkv_g2, rwkv_k_k, rwkv_k_a, rwkv_r_k,
           rwkv_lnx_w, rwkv_lnx_b, attn_w_q, attn_w_o, kv_w_k, kv_w_v, ffn_w_gate, ffn_w_up, ffn_w_down,
           ple_w_proj, ple_w_gate):
    assert norm_mix.shape[0] == 2 and state_wkv.shape[0] == 1, "one RWKV layer then one MoBA layer"
    bf = lambda w: w.astype(BF16)
    W = dict(norm_mix=norm_mix, norm_ffn=norm_ffn, norm_ple=norm_ple, norm_kv=norm_kv, norm_final=norm_final,
             rwkv_mu=rwkv_mu, w_r=bf(rwkv_w_rkv[0, 0]), w_k=bf(rwkv_w_rkv[0, 1]), w_v=bf(rwkv_w_rkv[0, 2]),
             rwkv_w_o=bf(rwkv_w_o[0]), rwkv_w0=rwkv_w0, rwkv_w1=bf(rwkv_w1[0]), rwkv_w2=bf(rwkv_w2[0]),
             rwkv_a0=rwkv_a0, rwkv_a1=bf(rwkv_a1[0]), rwkv_a2=bf(rwkv_a2[0]), rwkv_g1=bf(rwkv_g1[0]),
             rwkv_g2=bf(rwkv_g2[0]), rwkv_k_k=rwkv_k_k, rwkv_k_a=rwkv_k_a, rwkv_r_k=rwkv_r_k,
             rwkv_lnx_w=rwkv_lnx_w, rwkv_lnx_b=rwkv_lnx_b, attn_w_q=bf(attn_w_q[0]), attn_w_o=bf(attn_w_o[0]),
             kv_w_k=bf(kv_w_k), kv_w_v=bf(kv_w_v), kv_w_kT=bf(kv_w_k.T), kv_w_vT=bf(kv_w_v.T),
             ffn_w_gate=bf(ffn_w_gate), ffn_w_up=bf(ffn_w_up), ffn_w_down=bf(ffn_w_down),
             ple_w_proj=bf(ple_w_proj), ple_w_gate=bf(ple_w_gate))
    bp = x_prompt.shape[0]
    d = x_prompt.shape[2]
    wkv0 = jnp.zeros((bp, N_HEADS, HEAD_DIM, HEAD_DIM), F32)
    shift0 = jnp.zeros((bp, d), F32)
    y_p, wkv_p, shift_p, k_p, v_p = _trunk(x_prompt, p_prompt, wkv0, shift0, W)
    y_s, wkv_s, shift_s, k_s, v_s = _trunk(x_sample, p_sample, state_wkv[0], state_shift[0], W,
                                           cache=(cache_k, cache_v, page_table))
    return (y_p, y_s, wkv_p, shift_p, k_p, v_p, wkv_s, shift_s, k_s, v_s)
```

```python
import functools
import math

import jax
import jax.numpy as jnp
from jax import lax
from jax.experimental import pallas as pl
from jax.experimental.pallas import tpu as pltpu

F32, BF16 = jnp.float32, jnp.bfloat16
N_HEADS = 16
HEAD_DIM = 64
LANES = 128
MOBA_BLOCK = 256
MOBA_TOPK = 3
PAGE_SIZE = 128
RMS_EPS = 1e-6
LNX_EPS = 64e-5
NEG = -1e30
VMEM_LIMIT_BYTES = 56 * 1024 * 1024
SCAN_CHUNK = 64
FFN_TILE_MAX = 4096
POS_LANES = 3
UNROLL = 4
VT_ROWS = 80


def _cparams(*sem):
    return pltpu.CompilerParams(dimension_semantics=sem, vmem_limit_bytes=VMEM_LIMIT_BYTES)


def _rms(x, g):
    return x * lax.rsqrt(jnp.mean(x * x, axis=-1, keepdims=True) + RMS_EPS) * g


def _sigmoid(x):
    return 1.0 / (1.0 + jnp.exp(-x))


_NN = (((1,), (0,)), ((), ()))
_NT = (((1,), (1,)), ((), ()))


def _dot(a, b, dims=_NN):
    return lax.dot_general(a, b, dims, preferred_element_type=F32)


def _bf(x):
    return x.astype(BF16)


def _bdot(a, w):
    return _dot(a.astype(BF16), w)


def _split2(x):
    hi = x.astype(BF16)
    lo = (x - hi.astype(F32)).astype(BF16)
    return hi, lo


def _dot3(a, b, dims=_NN):
    ah, al = _split2(a)
    bh, bl = _split2(b)
    return _dot(ah, bh, dims) + (_dot(ah, bl, dims) + _dot(al, bh, dims))


def _full(shape):
    n = len(shape)
    return pl.BlockSpec(shape, lambda *_: (0,) * n)


def _rms_kernel(x_ref, g_ref, o_ref):
    o_ref[...] = _rms(x_ref[...], g_ref[...])


def rmsnorm_call(x, g, tm):
    t, d = x.shape
    row = pl.BlockSpec((tm, d), lambda i: (i, 0))
    return pl.pallas_call(
        _rms_kernel, grid=(t // tm,), in_specs=[row, _full((1, d))], out_specs=row,
        out_shape=jax.ShapeDtypeStruct((t, d), F32), compiler_params=_cparams("parallel"),
        name="rmsnorm")(x, g)


def _rwkv_proj_kernel(x_ref, xp_ref, mu_ref, wr, wk, wv, w1, w2, a1, a2, g1, g2, w0, a0, *rest, tiles_per_seq):
    if tiles_per_seq:
        gn_ref, sh_ref, r_o, k_o, v_o, lw_o, a_o, g_o, last_o = rest
        hn = _rms(x_ref[...], gn_ref[...])
        last_o[...] = hn[hn.shape[0] - 8:, :]
        prev_last = _rms(xp_ref[...], gn_ref[...])[7:8, :]
        seq_start = pl.program_id(0) % tiles_per_seq == 0
        carry = jnp.where(seq_start, sh_ref[...], prev_last)
        rowi = lax.broadcasted_iota(jnp.int32, hn.shape, 0)
        xp = jnp.where(rowi == 0, carry, pltpu.roll(hn, 1, 0))
    else:
        r_o, k_o, v_o, lw_o, a_o, g_o = rest
        hn = x_ref[...]
        xp = xp_ref[...]
    xx = xp - hn

    def mix(i):
        return hn + xx * mu_ref[i:i + 1, :]

    r_o[...] = _bdot(mix(0), wr[...])
    k_o[...] = _bdot(mix(2), wk[...])
    v_o[...] = _bdot(mix(3), wv[...])
    u = w0[...] + _bdot(jnp.tanh(_bdot(mix(1), w1[...])), w2[...])
    lw_o[...] = -math.exp(-0.5) * _sigmoid(u)
    a_o[...] = _sigmoid(a0[...] + _bdot(_bdot(mix(4), a1[...]), a2[...]))
    g_o[...] = _bdot(_sigmoid(_bdot(mix(5), g1[...])), g2[...])


def rwkv_proj_call(x, xp, mu, wr, wk, wv, w1, w2, a1, a2, g1, g2, w0, a0, tm, norm=None):
    t, d = x.shape
    row = pl.BlockSpec((tm, d), lambda i: (i, 0))
    ws = [mu, wr, wk, wv, w1, w2, a1, a2, g1, g2, w0, a0]
    outs = [jax.ShapeDtypeStruct((t, d), F32)] * 6
    ospecs = [row] * 6
    if norm is None:
        ins, specs, tps = [x, xp], [row, row], 0
    else:
        gain, shift = norm
        nseq = shift.shape[0]
        tps = t // tm // nseq
        ins = [x, x]
        specs = [row, pl.BlockSpec((8, d), lambda i: (jnp.maximum(i * (tm // 8) - 1, 0), 0))]
        ws = ws + [gain, shift]
        outs = outs + [jax.ShapeDtypeStruct((nseq, 8, d), F32)]
        ospecs = ospecs + [pl.BlockSpec((None, 8, d), lambda i: (i // tps, 0, 0))]
    wspecs = [_full(w.shape) for w in ws]
    if norm is not None:
        wspecs[-1] = pl.BlockSpec((None, 1, d), lambda i: (i // tps, 0, 0))
    return pl.pallas_call(
        functools.partial(_rwkv_proj_kernel, tiles_per_seq=tps), grid=(t // tm,),
        in_specs=specs + wspecs, out_specs=ospecs, out_shape=outs,
        compiler_params=_cparams("arbitrary"), name="rwkv_proj")(*ins, *ws)


def _post_mixer_kernel(x_ref, wo_ref, h_ref, gf_ref, wg_ref, wu_ref, wd_ref, p_ref, gp_ref, pg_ref, pp_ref,
                       *rest, final):
    if final:
        gfin_ref, o_ref, xn_scr = rest
    else:
        o_ref, xn_scr = rest
    f = pl.program_id(1)

    @pl.when(f == 0)
    def _():
        h1 = h_ref[...] + _bdot(x_ref[...], wo_ref[...])
        xn_scr[...] = _rms(h1, gf_ref[...]).astype(BF16)
        o_ref[...] = h1

    xn = xn_scr[...]
    gt = _dot(xn, wg_ref[...])
    up = _dot(xn, wu_ref[...])
    o_ref[...] += _bdot(gt * _sigmoid(gt) * up, wd_ref[...])

    @pl.when(f == pl.num_programs(1) - 1)
    def _():
        h2 = o_ref[...]
        gate = _sigmoid(_bdot(_rms(h2, gp_ref[...]), pg_ref[...]))
        h3 = h2 + _bdot(p_ref[...], pp_ref[...]) * gate
        o_ref[...] = _rms(h3, gfin_ref[...]) if final else h3


def post_mixer_call(x, wo, h, g_ffn, wg, wu, wd, p, g_ple, ple_wg, ple_wp, tm, tf, g_final=None):
    t, d = h.shape
    f = wg.shape[1]
    row = pl.BlockSpec((tm, d), lambda i, j: (i, 0))
    const = lambda shape: pl.BlockSpec(shape, lambda i, j: (0,) * len(shape), pipeline_mode=pl.Buffered(1))
    ins = [x, wo, h, g_ffn, wg, wu, wd, p, g_ple, ple_wg, ple_wp]
    specs = [row, const(wo.shape), row, const((1, d)),
             pl.BlockSpec((d, tf), lambda i, j: (0, j), pipeline_mode=pl.Buffered(1)), pl.BlockSpec((d, tf), lambda i, j: (0, j), pipeline_mode=pl.Buffered(1)),
             pl.BlockSpec((tf, d), lambda i, j: (j, 0), pipeline_mode=pl.Buffered(1)),
             pl.BlockSpec((tm, p.shape[1]), lambda i, j: (i, 0)), const((1, d)),
             const(ple_wg.shape), const(ple_wp.shape)]
    if g_final is not None:
        ins.append(g_final)
        specs.append(const((1, d)))
    return pl.pallas_call(
        functools.partial(_post_mixer_kernel, final=g_final is not None), grid=(t // tm, f // tf),
        in_specs=specs, out_specs=row, out_shape=jax.ShapeDtypeStruct((t, d), F32),
        scratch_shapes=[pltpu.VMEM((tm, d), BF16)],
        compiler_params=_cparams("parallel", "arbitrary"), name="post_mixer")(*ins)


def _kvq_kernel(h_ref, gkv_ref, gq_ref, wk_ref, wv_ref, wq_ref, *rest, prompt):
    h = h_ref[...]
    hk = _rms(h, gkv_ref[...]).astype(BF16)
    k = _dot(hk, wk_ref[...])
    if not prompt:
        k_o, v_o, q_o = rest
        k_o[...] = k
        v_o[...] = _dot(hk, wv_ref[...])
        q_o[...] = _bdot(_rms(h, gq_ref[...]), wq_ref[...])
        return
    wkt_ref, kt_o, vt_o, q_o, kb_o, vtb_o, km_o = rest
    q_o[...] = _bdot(_rms(h, gq_ref[...]), wq_ref[...])
    kt_o[...] = _dot(wkt_ref[...], hk, _NT)
    vt = _dot(wv_ref[...], hk, _NT)
    vt_o[...] = vt
    tm, d = k.shape
    lane = lax.broadcasted_iota(jnp.int32, (tm, LANES), 1)
    rpos = lax.broadcasted_iota(jnp.int32, (tm, LANES), 0).astype(F32)
    for p in range(d // LANES):
        kp = k[:, p * LANES:(p + 1) * LANES]
        for hh in range(2):
            own = (lane < HEAD_DIM) == (hh == 0)
            pos = lane - (HEAD_DIM if hh == 0 else 0)
            extra = jnp.where(jnp.logical_and(pos >= 0, pos < POS_LANES), rpos, 0.0)
            c0 = (2 * p + hh) * LANES
            kb_o[:, c0:c0 + LANES] = jnp.where(own, kp, extra).astype(BF16)
    ones_rows = jnp.where(lax.broadcasted_iota(jnp.int32, (VT_ROWS - HEAD_DIM, tm), 0) == 0, 1.0, 0.0)
    for hd in range(d // HEAD_DIM):
        vtb_o[hd * VT_ROWS:hd * VT_ROWS + HEAD_DIM, :] = vt[hd * HEAD_DIM:(hd + 1) * HEAD_DIM, :].astype(BF16)
        vtb_o[hd * VT_ROWS + HEAD_DIM:(hd + 1) * VT_ROWS, :] = ones_rows.astype(BF16)
    km_o[...] = jnp.sum(k, axis=0, keepdims=True) * (1.0 / MOBA_BLOCK)


def kvq_call(h, gkv, gq, wk, wv, wq, tm, wkt=None, seq=None):
    t, d = h.shape
    prompt = wkt is not None
    row = pl.BlockSpec((tm, d), lambda i: (i, 0))
    ins = [h, gkv, gq, wk, wv, wq]
    specs = [row, _full(gkv.shape), _full(gq.shape), _full(wk.shape), _full(wv.shape), _full(wq.shape)]
    if prompt:
        assert tm == MOBA_BLOCK
        tps = seq // tm
        nh = d // HEAD_DIM
        tspec = lambda rows: pl.BlockSpec((None, rows, tm), lambda i: (i // tps, 0, i % tps))
        ins.append(wkt)
        specs.append(_full(wkt.shape))
        outs = [jax.ShapeDtypeStruct((t // seq, d, seq), F32)] * 2 + [
            jax.ShapeDtypeStruct((t, d), F32),
            jax.ShapeDtypeStruct((t, nh * LANES), BF16),
            jax.ShapeDtypeStruct((t // seq, nh * VT_ROWS, seq), BF16),
            jax.ShapeDtypeStruct((t // tm, 1, d), F32)]
        ospecs = [tspec(d), tspec(d), row, pl.BlockSpec((tm, nh * LANES), lambda i: (i, 0)),
                  tspec(nh * VT_ROWS), pl.BlockSpec((None, 1, d), lambda i: (i, 0, 0))]
    else:
        outs = [jax.ShapeDtypeStruct((t, d), F32)] * 3
        ospecs = [row] * 3
    return pl.pallas_call(
        functools.partial(_kvq_kernel, prompt=prompt), grid=(t // tm,), in_specs=specs,
        out_specs=ospecs, out_shape=outs, compiler_params=_cparams("parallel"), name="kvq")(*ins)


def _scan_kernel(r_ref, k_ref, v_ref, lw_ref, a_ref, g_ref, kk_ref, ka_ref, rk_ref, lnw_ref, lnb_ref,
                 s0_ref, o_ref, sout_ref, z_scr, *, C, NP, NCH):
    c = pl.program_id(2)
    pairs = range(NP)

    def split3(x):
        x1 = x.astype(BF16)
        r1 = x - x1.astype(F32)
        x2 = r1.astype(BF16)
        return x1, x2, (r1 - x2.astype(F32)).astype(BF16)

    sel_r = lax.broadcasted_iota(jnp.int32, (LANES, HEAD_DIM), 0)
    sel_c = lax.broadcasted_iota(jnp.int32, (LANES, HEAD_DIM), 1)
    place = [jnp.where(sel_r == sel_c + h * HEAD_DIM, 1.0, 0.0).astype(BF16) for h in range(2)]
    pk_r = lax.broadcasted_iota(jnp.int32, (HEAD_DIM, LANES), 0)
    pk_c = lax.broadcasted_iota(jnp.int32, (HEAD_DIM, LANES), 1)
    pick = [jnp.where(pk_c == pk_r + h * HEAD_DIM, 1.0, 0.0).astype(BF16) for h in range(2)]
    zero_s = jnp.zeros((HEAD_DIM, HEAD_DIM), BF16)

    @pl.when(c == 0)
    def _():
        for p in pairs:
            z = jnp.zeros((LANES, LANES), F32)
            for h in range(2):
                for part in split3(s0_ref[2 * p + h]):
                    rows = [part, zero_s] if h == 0 else [zero_s, part]
                    z = z + _dot(place[h], jnp.concatenate(rows, axis=0), _NT)
            z_scr[p] = z

    lane = lax.broadcasted_iota(jnp.int32, (1, LANES), 1)
    m0 = lane < HEAD_DIM
    masks = (m0, jnp.logical_not(m0))
    units = [(ch, p) for ch in range(NCH) for p in pairs]
    heads = [(u, h) for u in range(len(units)) for h in range(2)]

    def hsum(x):
        s0 = jnp.sum(jnp.where(m0, x, 0.0), axis=-1, keepdims=True)
        s1 = jnp.sum(jnp.where(m0, 0.0, x), axis=-1, keepdims=True)
        return jnp.where(m0, s0, s1)

    def tile(ref, u):
        ch, p = units[u]
        return ref[ch * C:(ch + 1) * C, p * LANES:(p + 1) * LANES]

    def ptile(ref, u):
        p = units[u][1]
        return ref[:, p * LANES:(p + 1) * LANES]

    ri = lax.broadcasted_iota(jnp.int32, (C, C), 0)
    ci = lax.broadcasted_iota(jnp.int32, (C, C), 1)
    ltri = jnp.where(ci <= ri, 1.0, 0.0).astype(BF16)
    strict_c = ci < ri
    incl_c = ci <= ri
    rw = lax.broadcasted_iota(jnp.int32, (C, 2 * C), 0)
    cw = lax.broadcasted_iota(jnp.int32, (C, 2 * C), 1)
    strict_w = jnp.where(cw < C, cw, cw - C) < rw
    right_w = cw >= C
    eye_w = cw - C == rw
    zero_c = jnp.zeros((C, LANES), BF16)
    rz = lax.broadcasted_iota(jnp.int32, (LANES, LANES), 0)
    cz = lax.broadcasted_iota(jnp.int32, (LANES, LANES), 1)
    same_head = (rz < HEAD_DIM) == (cz < HEAD_DIM)

    nu = range(len(units))

    r = [tile(r_ref, u) for u in nu]
    k = [tile(k_ref, u) for u in nu]
    v = [tile(v_ref, u) for u in nu]
    lw = [tile(lw_ref, u) for u in nu]
    a = [tile(a_ref, u) for u in nu]

    def cumsum(x):
        l1 = x.astype(BF16)
        rem = x - l1.astype(F32)
        l2 = rem.astype(BF16)
        l3 = (rem - l2.astype(F32)).astype(BF16)
        return _dot(ltri, l1) + (_dot(ltri, l2) + _dot(ltri, l3))

    cl = [cumsum(lw[u]) for u in nu]

    kmod, ars, rhs_cols, bk_t, gc_col, vs = [], [], [], [], [], []
    for u in nu:
        kkr = k[u] * ptile(kk_ref, u)
        kk = kkr / jnp.maximum(jnp.sqrt(hsum(kkr * kkr)), 1e-12)
        beta = kk * a[u]
        km = k[u] * (1.0 + (a[u] - 1.0) * ptile(ka_ref, u))
        kmod.append(km)
        igam = jnp.exp(-cl[u])
        a_t = -kk * jnp.exp(cl[u] - lw[u])
        r_t = r[u] * jnp.exp(cl[u])
        b_h = beta * igam
        k_h = km * igam
        gout = jnp.exp(cl[u][C - 1:C, :] - cl[u])
        ars.append(_bf(jnp.concatenate([a_t, r_t], axis=0)))
        rhs_cols.append(_bf(jnp.concatenate([b_h, b_h, k_h], axis=0)))
        bk_t.append(_bf(jnp.concatenate([(beta * gout).T, (km * gout).T], axis=1)))
        gc_col.append(jnp.exp(jnp.sum(lw[u].T, axis=1, keepdims=True)))
        vs.append(_bf(v[u]))

    gfull = [_dot(jnp.where(masks[h], ars[u], jnp.zeros_like(ars[u])), rhs_cols[u], _NT)
             for u, h in heads]
    w = [jnp.where(strict_w, g[:C, :2 * C], 0.0) for g in gfull]
    gk = [_bf(jnp.concatenate([jnp.where(strict_c, g[:C, 2 * C:], 0.0),
                               jnp.where(incl_c, g[C:, 2 * C:], 0.0)], axis=0)) for g in gfull]
    arb = [_bf(jnp.where(incl_c, g[C:, :C], 0.0)) for g in gfull]
    av = [_dot(gk[i], vs[u]) for i, (u, h) in enumerate(heads)]
    avs = [jnp.where(m0, av[2 * u], av[2 * u + 1]) for u in nu]

    for _ in range(int(math.log2(C))):
        wb = [_bf(x) for x in w]
        w = [jnp.where(right_w, w[i], 0.0) + _dot(wb[i][:, :C], wb[i]) for i in range(len(heads))]
    minv = [_bf(jnp.where(right_w, x, 0.0) + jnp.where(eye_w, 1.0, 0.0)) for x in w]

    z = [z_scr[p] for p in pairs]
    inv_n = 1.0 / HEAD_DIM
    for ch in range(NCH):
        us = [ch * NP + p for p in pairs]
        xs = [_dot(ars[u], _bf(z[p])) for p, u in zip(pairs, us)]
        rhs = [jnp.concatenate([zero_c, _bf(x[:C] + avs[u][:C])], axis=0) for x, u in zip(xs, us)]
        uh = [_dot(minv[2 * u + h], rhs[p]) for p, u in zip(pairs, us) for h in range(2)]
        ub = [_bf(jnp.where(m0, uh[2 * p], uh[2 * p + 1])) for p in pairs]
        oh = [_dot(arb[2 * u + h], ub[p]) for p, u in zip(pairs, us) for h in range(2)]
        zd = [_dot(bk_t[u], jnp.concatenate([ub[p], vs[u]], axis=0)) for p, u in zip(pairs, us)]
        z = [jnp.where(same_head, z[p] * gc_col[u] + zd[p], 0.0) for p, u in zip(pairs, us)]
        for p, u in zip(pairs, us):
            o = xs[p][C:] + avs[u][C:] + jnp.where(m0, oh[2 * p], oh[2 * p + 1])
            mean = hsum(o) * inv_n
            dlt = o - mean
            var = hsum(dlt * dlt) * inv_n
            on = dlt * lax.rsqrt(var + LNX_EPS) * ptile(lnw_ref, u) + ptile(lnb_ref, u)
            bonus = hsum(r[u] * kmod[u] * ptile(rk_ref, u)) * v[u]
            o_ref[ch * C:(ch + 1) * C, p * LANES:(p + 1) * LANES] = (on + bonus) * tile(g_ref, u)

    for p in pairs:
        z_scr[p] = z[p]

        @pl.when(c == pl.num_programs(2) - 1)
        def _():
            for h in range(2):
                out = jnp.zeros((HEAD_DIM, HEAD_DIM), F32)
                for part in split3(z[p][h * HEAD_DIM:(h + 1) * HEAD_DIM, :]):
                    out = out + _dot(pick[h], part, _NT)
                sout_ref[2 * p + h] = out


def scan_call(r, k, v, lw, a, g, k_k, k_a, r_k, lnw, lnb, s0, nb, seq, C, NP, NCH):
    t, d = r.shape
    npairs = d // LANES
    rows = C * NCH
    nc = seq // rows
    w = NP * LANES
    row = pl.BlockSpec((rows, w), lambda b, p, c: (b * nc + c, p))
    par = pl.BlockSpec((1, w), lambda b, p, c: (0, p))
    sspec = pl.BlockSpec((None, 2 * NP, HEAD_DIM, HEAD_DIM), lambda b, p, c: (b, p, 0, 0))
    return pl.pallas_call(
        functools.partial(_scan_kernel, C=C, NP=NP, NCH=NCH), grid=(nb, npairs // NP, nc),
        in_specs=[row] * 6 + [par] * 5 + [sspec],
        out_specs=[row, sspec],
        out_shape=[jax.ShapeDtypeStruct((t, d), F32),
                   jax.ShapeDtypeStruct((nb, 2 * npairs, HEAD_DIM, HEAD_DIM), F32)],
        scratch_shapes=[pltpu.VMEM((NP, LANES, LANES), F32)],
        compiler_params=_cparams("parallel", "parallel", "arbitrary"), name="rwkv_scan")(
            r, k, v, lw, a, g, k_k, k_a, r_k, lnw, lnb, s0)


def _top3_rows(gate, rowi, nrow):
    sel = jnp.zeros(gate.shape, jnp.bool_)
    for _ in range(MOBA_TOPK):
        mx = jnp.max(gate, axis=0, keepdims=True)
        idx = jnp.min(jnp.where(gate == mx, rowi, nrow), axis=0, keepdims=True)
        pick = jnp.logical_and(rowi == idx, mx > -jnp.inf)
        sel = jnp.logical_or(sel, pick)
        gate = jnp.where(pick, -jnp.inf, gate)
    return sel


def _moba_prompt_kernel(sl_ref, q_ref, k_ref, vt_ref, km_ref, o_ref, sel_scr, sa_scr, sb_scr, pa_scr, pb_scr,
                        *, nb):
    p = pl.program_id(1)
    i = pl.program_id(2)
    blk = MOBA_BLOCK
    lane = lax.broadcasted_iota(jnp.int32, (1, LANES), 1)
    m0 = lane < HEAD_DIM
    masks = (m0, jnp.logical_not(m0))
    q = q_ref[...]
    km = km_ref[...]
    rowi = lax.broadcasted_iota(jnp.int32, (nb, blk), 0)
    causal = (lax.broadcasted_iota(jnp.int32, (blk, blk), 1)
              >= lax.broadcasted_iota(jnp.int32, (blk, blk), 0))
    log2e = 1.0 / math.log(2.0)
    scale2 = log2e / math.sqrt(HEAD_DIM)

    qb, slopes = [], []
    for h in range(2):
        qh = jnp.where(masks[h], q, 0.0)
        gate = jnp.where(rowi < i, _dot3(km, qh, _NT), -jnp.inf)
        sel_scr[h] = jnp.where(_top3_rows(gate, rowi, nb), 0.0, NEG)
        slope = sl_ref[2 * p + h] * log2e
        slopes.append(slope)
        pos = lane - (HEAD_DIM if h == 0 else 0)
        rest = jnp.full((1, LANES), slope, F32)
        terms = jnp.zeros((1, LANES), F32)
        for t in range(POS_LANES):
            part = rest.astype(BF16).astype(F32)
            terms = jnp.where(pos == t, part, terms)
            rest = rest - part
        qb.append(jnp.where(masks[h], q * scale2, terms).astype(BF16))

    def block_off(jb):
        return pl.multiple_of(jnp.minimum(jb, nb - 1) * blk, blk)

    def scores_into(dst, jb):
        kt = k_ref[pl.ds(block_off(jb), blk), :]
        for h in range(2):
            dst[h] = _dot(kt[:, h * LANES:(h + 1) * LANES], qb[h], _NT)

    def pv_of(p_src, jb):
        vt = vt_ref[:, pl.ds(block_off(jb), blk)]
        return [_dot(vt[h * VT_ROWS:(h + 1) * VT_ROWS, :], p_src[h]) for h in range(2)]

    def step(s_src, p_dst, rowvec, pend, carry, own):
        pv = None if own else pv_of(*pend)
        new = []
        for h in range(2):
            m, a_prev, acc = carry[h]
            s = s_src[h]
            if own:
                s = jnp.where(causal, s, NEG)
            m_new = jnp.maximum(m, jnp.max(s, axis=0, keepdims=True) + rowvec[h])
            p_dst[h] = jnp.exp2(s - (m_new - rowvec[h])).astype(BF16)
            new.append((m_new, jnp.exp2(m - m_new), acc if own else a_prev * acc + pv[h]))
        return tuple(new)

    def past_rowvec(jb):
        gap = ((i - jb) * blk).astype(F32)
        jc = jnp.minimum(jb, nb - 1)
        return [jnp.where(jb < i, sel_scr[h, pl.ds(jc, 1), :] - slopes[h] * gap, NEG) for h in range(2)]

    scores_into(sb_scr, i)
    scores_into(sa_scr, 0)
    zero_row = jnp.zeros((1, blk), F32)
    init = tuple((jnp.full((1, blk), NEG, F32), zero_row, jnp.zeros((VT_ROWS, blk), F32)) for _ in range(2))
    carry = step(sb_scr, pb_scr, (zero_row, zero_row), None, init, True)

    def body(jj, carry):
        j0 = UNROLL * jj
        for u in range(0, UNROLL, 2):
            scores_into(sb_scr, j0 + u + 1)
            prev = jnp.where(jj == 0, i, j0 - 1) if u == 0 else j0 + u - 1
            carry = step(sa_scr, pa_scr, past_rowvec(j0 + u), (pb_scr, prev), carry, False)
            scores_into(sa_scr, j0 + u + 2)
            carry = step(sb_scr, pb_scr, past_rowvec(j0 + u + 1), (pa_scr, j0 + u), carry, False)
        return carry

    trips = (i + UNROLL - 1) // UNROLL
    carry = lax.fori_loop(0, trips, body, carry)
    pv = pv_of(pb_scr, jnp.where(trips == 0, i, UNROLL * trips - 1))
    acc = [carry[h][1] * carry[h][2] + pv[h] for h in range(2)]
    out_t = jnp.concatenate([acc[h][:HEAD_DIM] / acc[h][HEAD_DIM:HEAD_DIM + 1] for h in range(2)], axis=0)
    o_ref[...] = out_t.T


def moba_prompt_call(slopes, q, kb, vt, kmean, nbatch, seq):
    t, d = q.shape
    nb = seq // MOBA_BLOCK
    npairs = d // LANES
    qspec = pl.BlockSpec((MOBA_BLOCK, LANES), lambda b, p, i: (b * nb + i, p))
    tile = pltpu.VMEM((2, MOBA_BLOCK, MOBA_BLOCK), F32)
    ptile = pltpu.VMEM((2, MOBA_BLOCK, MOBA_BLOCK), BF16)
    return pl.pallas_call(
        functools.partial(_moba_prompt_kernel, nb=nb), grid=(nbatch, npairs, nb),
        in_specs=[pl.BlockSpec(memory_space=pltpu.SMEM), qspec,
                  pl.BlockSpec((seq, 2 * LANES), lambda b, p, i: (b, p)),
                  pl.BlockSpec((None, 2 * VT_ROWS, seq), lambda b, p, i: (b, p, 0)),
                  pl.BlockSpec((None, nb, LANES), lambda b, p, i: (b, 0, p))],
        out_specs=qspec, out_shape=jax.ShapeDtypeStruct((t, d), F32),
        scratch_shapes=[pltpu.VMEM((2, nb, MOBA_BLOCK), F32), tile, tile, ptile, ptile],
        compiler_params=_cparams("parallel", "parallel", "arbitrary"), name="moba_prompt")(
            slopes, q, kb, vt, kmean)


def _head_tiles(page_refs, h):
    return jnp.concatenate([ref[h] for ref in page_refs], axis=1).astype(BF16)


def _sample_scores_kernel(pt_ref, q_ref, *refs, bps):
    del pt_ref
    page_refs, (sc_ref, g_ref) = refs[:2 * bps], refs[2 * bps:]
    j = pl.program_id(1)
    rows = q_ref.shape[1]
    lane = lax.broadcasted_iota(jnp.int32, g_ref.shape[1:], 1)

    @pl.when(j == 0)
    def _():
        g_ref[...] = jnp.zeros(g_ref.shape, F32)

    for h in range(N_HEADS):
        q = q_ref[h]
        q1 = q.astype(BF16)
        r1 = q - q1.astype(F32)
        q2 = r1.astype(BF16)
        q3 = (r1 - q2.astype(F32)).astype(BF16)
        s3 = _dot(jnp.concatenate([q1, q2, q3], axis=0), _head_tiles(page_refs, h))
        sc_ref[h] = s3[:rows]
        full = s3[:rows] + (s3[rows:2 * rows] + s3[2 * rows:])
        g = g_ref[h]
        for s in range(bps):
            gcol = jnp.sum(full[:, s * MOBA_BLOCK:(s + 1) * MOBA_BLOCK], axis=1, keepdims=True)
            g = jnp.where(lane == j * bps + s, gcol * (1.0 / MOBA_BLOCK), g)
        g_ref[h] = g


def _paged_specs(page_table, page_shape, bps):
    npg = page_table.shape[1]
    ppb = MOBA_BLOCK // PAGE_SIZE
    assert ppb == 2
    return [pl.BlockSpec((None,) + page_shape,
                         functools.partial(lambda b, j, pt, o: (pt[b * npg + ppb * bps * j + o], 0, 0, 0), o=o))
            for o in range(ppb * bps)]


def _blocks_per_step(nbp):
    return max(n for n in (4, 2, 1) if nbp % n == 0)


def sample_scores_call(page_table, q4, cache_kt):
    bs, nh, rows, dh = q4.shape
    nbp = page_table.shape[1] * PAGE_SIZE // MOBA_BLOCK
    bps = _blocks_per_step(nbp)
    per_b = lambda last: pl.BlockSpec((None, nh, rows, last), lambda b, j, pt: (b, 0, 0, 0))
    gs = pltpu.PrefetchScalarGridSpec(
        num_scalar_prefetch=1, grid=(bs, nbp // bps),
        in_specs=[per_b(dh)] + _paged_specs(page_table, (nh, dh, PAGE_SIZE), bps),
        out_specs=[pl.BlockSpec((None, nh, rows, bps * MOBA_BLOCK), lambda b, j, pt: (b, 0, 0, j)), per_b(nbp)])
    return pl.pallas_call(
        functools.partial(_sample_scores_kernel, bps=bps), grid_spec=gs,
        out_shape=[jax.ShapeDtypeStruct((bs, nh, rows, nbp * MOBA_BLOCK), F32),
                   jax.ShapeDtypeStruct((bs, nh, rows, nbp), F32)],
        compiler_params=_cparams("parallel", "arbitrary"), name="sample_scores")(
            page_table.reshape(-1), q4, *([cache_kt] * (2 * bps)))


def _sample_attend_kernel(pt_ref, sl_ref, sc_ref, g_ref, q_ref, ko_ref, vo_ref, *refs, nbp, bps, past_len):
    del pt_ref
    page_refs, (o_ref, m_scr, l_scr, acc_scr, sel_scr) = refs[:2 * bps], refs[2 * bps:]
    j = pl.program_id(1)
    rows = q_ref.shape[1]
    scale = 1.0 / math.sqrt(HEAD_DIM)
    tq = lax.broadcasted_iota(jnp.int32, (rows, 1), 0)
    heads = range(N_HEADS)

    @pl.when(j == 0)
    def _():
        lane = lax.broadcasted_iota(jnp.int32, (rows, nbp), 1)
        tk = lax.broadcasted_iota(jnp.int32, (rows, rows), 1)
        for h in heads:
            gate = g_ref[h]
            sel = jnp.zeros(gate.shape, jnp.bool_)
            for _ in range(MOBA_TOPK):
                mx = jnp.max(gate, axis=1, keepdims=True)
                idx = jnp.min(jnp.where(gate == mx, lane, nbp), axis=1, keepdims=True)
                pick = jnp.logical_and(lane == idx, mx > -jnp.inf)
                sel = jnp.logical_or(sel, pick)
                gate = jnp.where(pick, -jnp.inf, gate)
            sel_scr[h] = jnp.where(sel, 0.0, NEG)
            s = _dot(q_ref[h].astype(BF16), ko_ref[h].astype(BF16), _NT) * scale
            s = jnp.where(tk <= tq, s - sl_ref[h] * (tq - tk).astype(F32), NEG)
            m = jnp.max(s, axis=1, keepdims=True)
            pr = jnp.exp(s - m)
            m_scr[h] = m
            l_scr[h] = jnp.sum(pr, axis=1, keepdims=True)
            acc_scr[h] = _dot(pr.astype(BF16), vo_ref[h].astype(BF16))

    width = bps * MOBA_BLOCK
    lane_k = lax.broadcasted_iota(jnp.int32, (rows, width), 1)
    lane_b = lax.broadcasted_iota(jnp.int32, (rows, nbp), 1)
    dist = ((past_len + tq) - (j * width + lane_k)).astype(F32)
    prs, alphas = [], []
    for h in heads:
        sel = sel_scr[h]
        bias = jnp.sum(jnp.where(lane_b == j * bps, sel, 0.0), axis=1, keepdims=True)
        for sb in range(1, bps):
            bias_sb = jnp.sum(jnp.where(lane_b == j * bps + sb, sel, 0.0), axis=1, keepdims=True)
            bias = jnp.where(lane_k >= sb * MOBA_BLOCK, bias_sb, bias)
        s = sc_ref[h] * scale - sl_ref[h] * dist + bias
        m_old = m_scr[h]
        m_new = jnp.maximum(m_old, jnp.max(s, axis=1, keepdims=True))
        alpha = jnp.exp(m_old - m_new)
        pr = jnp.exp(s - m_new)
        m_scr[h] = m_new
        l_scr[h] = alpha * l_scr[h] + jnp.sum(pr, axis=1, keepdims=True)
        prs.append(pr.astype(BF16))
        alphas.append(alpha)
    for h in heads:
        acc_scr[h] = alphas[h] * acc_scr[h] + _dot(prs[h], _head_tiles(page_refs, h), _NT)

    @pl.when(j == pl.num_programs(1) - 1)
    def _():
        for h in heads:
            o_ref[h] = acc_scr[h] / l_scr[h]


def sample_attend_call(page_table, slopes, scores, gate, q4, k_own, v_own, cache_vt):
    bs, nh, rows, dh = q4.shape
    npg = page_table.shape[1]
    nbp = npg * PAGE_SIZE // MOBA_BLOCK
    bps = _blocks_per_step(nbp)
    per_b = lambda last: pl.BlockSpec((None, nh, rows, last), lambda b, j, pt: (b, 0, 0, 0))
    gs = pltpu.PrefetchScalarGridSpec(
        num_scalar_prefetch=1, grid=(bs, nbp // bps),
        in_specs=[pl.BlockSpec(memory_space=pltpu.SMEM),
                  pl.BlockSpec((None, nh, rows, bps * MOBA_BLOCK), lambda b, j, pt: (b, 0, 0, j)),
                  per_b(nbp), per_b(dh), per_b(dh), per_b(dh)]
                 + _paged_specs(page_table, (nh, dh, PAGE_SIZE), bps),
        out_specs=per_b(dh),
        scratch_shapes=[pltpu.VMEM((nh, rows, 1), F32), pltpu.VMEM((nh, rows, 1), F32),
                        pltpu.VMEM((nh, rows, dh), F32), pltpu.VMEM((nh, rows, nbp), F32)])
    return pl.pallas_call(
        functools.partial(_sample_attend_kernel, nbp=nbp, bps=bps, past_len=npg * PAGE_SIZE),
        grid_spec=gs, out_shape=jax.ShapeDtypeStruct((bs, nh, rows, dh), F32),
        compiler_params=_cparams("parallel", "arbitrary"), name="sample_attend")(
            page_table.reshape(-1), slopes, scores, gate, q4, k_own, v_own, *([cache_vt] * (2 * bps)))


def _row_tile(t):
    for tm in (512, 256, 128):
        if t % tm == 0:
            return tm
    raise ValueError(f"token count {t} is not a multiple of 128")


def _ffn_tile(f):
    for nf in (1, 2, 4, 11, 22):
        if f % nf == 0 and (f // nf) % LANES == 0 and f // nf <= FFN_TILE_MAX:
            return f // nf
    raise ValueError(f"unsupported FFN width {f}")


def _trunk(x, p, wkv_in, shift_in, W, cache=None):
    b, t, d = x.shape
    nt = b * t
    tm = _row_tile(nt)
    h = x.reshape(nt, d)
    row = lambda v: v.reshape(1, -1)

    proj_w = (W['rwkv_mu'][0], W['w_r'], W['w_k'], W['w_v'], W['rwkv_w1'], W['rwkv_w2'],
              W['rwkv_a1'], W['rwkv_a2'], W['rwkv_g1'], W['rwkv_g2'], row(W['rwkv_w0'][0]), row(W['rwkv_a0'][0]))
    if t % tm == 0:
        r, k, v, lw, a, g, last = rwkv_proj_call(h, None, *proj_w, tm,
                                                 norm=(row(W['norm_mix'][0]), shift_in[:, None, :]))
        shift_out = last[:, -1]
    else:
        hn = rmsnorm_call(h, row(W['norm_mix'][0]), tm)
        hn3 = hn.reshape(b, t, d)
        xp = jnp.concatenate([shift_in[:, None, :], hn3[:, :-1]], axis=1).reshape(nt, d)
        shift_out = hn3[:, -1]
        r, k, v, lw, a, g = rwkv_proj_call(hn, xp, *proj_w, tm)
    chunk = SCAN_CHUNK if t % SCAN_CHUNK == 0 else 8
    tp = -(-t // chunk) * chunk
    seqs = (r, k, v, lw, a, g)
    if tp != t:
        seqs = tuple(jnp.pad(s.reshape(b, t, d), ((0, 0), (0, tp - t), (0, 0))).reshape(b * tp, d)
                     for s in seqs)
    o, wkv_out = scan_call(*seqs, row(W['rwkv_k_k'][0]), row(W['rwkv_k_a'][0]), row(W['rwkv_r_k'][0]),
                      row(W['rwkv_lnx_w'][0]), row(W['rwkv_lnx_b'][0]), wkv_in, b, tp, chunk,
                      d // LANES, 2 if tp % (2 * chunk) == 0 else 1)
    if tp != t:
        o = o.reshape(b, tp, d)[:, :t].reshape(nt, d)
    tf = _ffn_tile(W['ffn_w_gate'][0].shape[1])

    def post_mixer(x_mix, w_o, h, i, g_final=None):
        return post_mixer_call(x_mix, w_o, h, row(W['norm_ffn'][i]), W['ffn_w_gate'][i], W['ffn_w_up'][i],
                               W['ffn_w_down'][i], p[i].reshape(nt, -1), row(W['norm_ple'][i]),
                               W['ple_w_gate'][i], W['ple_w_proj'][i], tm, tf, g_final=g_final)

    h = post_mixer(o, W['rwkv_w_o'], h, 0)

    slopes = jnp.exp2(-8.0 * jnp.arange(1, N_HEADS + 1, dtype=F32) / N_HEADS)
    if cache is None:
        kt, vt, q, kb, vtb, kmean = kvq_call(
            h, row(W['norm_kv']), row(W['norm_mix'][1]), W['kv_w_k'], W['kv_w_vT'], W['attn_w_q'],
            MOBA_BLOCK, wkt=W['kv_w_kT'], seq=t)
        attn = moba_prompt_call(slopes, q, kb, vtb, kmean.reshape(b, t // MOBA_BLOCK, d), b, t)
        unT = lambda z: jnp.transpose(z.reshape(b, N_HEADS, HEAD_DIM, t), (0, 3, 1, 2))
        k_new, v_new = unT(kt), unT(vt)
    else:
        cache_k, cache_v, page_table = cache
        k_new, v_new, q = kvq_call(h, row(W['norm_kv']), row(W['norm_mix'][1]),
                                   W['kv_w_k'], W['kv_w_v'], W['attn_w_q'], tm)

        def per_head(z):
            z4 = jnp.swapaxes(z.reshape(b, t, N_HEADS, HEAD_DIM), 1, 2)
            return jnp.pad(z4, ((0, 0), (0, 0), (0, 8 - t), (0, 0)))

        q4 = per_head(q)
        pages_t = lambda c: jnp.transpose(c, (0, 2, 3, 1))
        scores, gate = sample_scores_call(page_table, q4, pages_t(cache_k))
        attn4 = sample_attend_call(page_table, slopes, scores, gate, q4, per_head(k_new), per_head(v_new),
                                   pages_t(cache_v))
        attn = jnp.swapaxes(attn4[:, :, :t], 1, 2).reshape(nt, d)
    y = post_mixer(attn, W['attn_w_o'], h, 1, g_final=row(W['norm_final']))
    shp = (b, t, N_HEADS, HEAD_DIM)
    return (y.reshape(b, t, d), wkv_out[None], shift_out[None], k_new.reshape(shp), v_new.reshape(shp))


def kernel(x_prompt, x_sample, p_prompt, p_sample, state_wkv, state_shift, cache_k, cache_v, page_table,
           norm_mix, norm_ffn, norm_ple, norm_kv, norm_final, rwkv_mu, rwkv_w_rkv, rwkv_w_o, rwkv_w0,
           rwkv_w1, rwkv_w2, rwkv_a0, rwkv_a1, rwkv_a2, rwkv_g1, rwkv_g2, rwkv_k_k, rwkv_k_a, rwkv_r_k,
           rwkv_lnx_w, rwkv_lnx_b, attn_w_q, attn_w_o, kv_w_k, kv_w_v, ffn_w_gate, ffn_w_up, ffn_w_down,
           ple_w_proj, ple_w_gate):
    assert norm_mix.shape[0] == 2 and state_wkv.shape[0] == 1, "one RWKV layer then one MoBA layer"
    bf = lambda w: w.astype(BF16)
    W = dict(norm_mix=norm_mix, norm_ffn=norm_ffn, norm_ple=norm_ple, norm_kv=norm_kv, norm_final=norm_final,
             rwkv_mu=rwkv_mu, w_r=bf(rwkv_w_rkv[0, 0]), w_k=bf(rwkv_w_rkv[0, 1]), w_v=bf(rwkv_w_rkv[0, 2]),
             rwkv_w_o=bf(rwkv_w_o[0]), rwkv_w0=rwkv_w0, rwkv_w1=bf(rwkv_w1[0]), rwkv_w2=bf(rwkv_w2[0]),
             rwkv_a0=rwkv_a0, rwkv_a1=bf(rwkv_a1[0]), rwkv_a2=bf(rwkv_a2[0]), rwkv_g1=bf(rwkv_g1[0]),
             rwkv_g2=bf(rwkv_g2[0]), rwkv_k_k=rwkv_k_k, rwkv_k_a=rwkv_k_a, rwkv_r_k=rwkv_r_k,
             rwkv_lnx_w=rwkv_lnx_w, rwkv_lnx_b=rwkv_lnx_b, attn_w_q=bf(attn_w_q[0]), attn_w_o=bf(attn_w_o[0]),
             kv_w_k=bf(kv_w_k), kv_w_v=bf(kv_w_v), kv_w_kT=bf(kv_w_k.T), kv_w_vT=bf(kv_w_v.T),
             ffn_w_gate=bf(ffn_w_gate), ffn_w_up=bf(ffn_w_up), ffn_w_down=bf(ffn_w_down),
             ple_w_proj=bf(ple_w_proj), ple_w_gate=bf(ple_w_gate))
    bp = x_prompt.shape[0]
    d = x_prompt.shape[2]
    wkv0 = jnp.zeros((bp, N_HEADS, HEAD_DIM, HEAD_DIM), F32)
    shift0 = jnp.zeros((bp, d), F32)
    y_p, wkv_p, shift_p, k_p, v_p = _trunk(x_prompt, p_prompt, wkv0, shift0, W)
    y_s, wkv_s, shift_s, k_s, v_s = _trunk(x_sample, p_sample, state_wkv[0], state_shift[0], W,
                                           cache=(cache_k, cache_v, page_table))
    return (y_p, y_s, wkv_p, shift_p, k_p, v_p, wkv_s, shift_s, k_s, v_s)
```

```python
import functools
import math

import jax
import jax.numpy as jnp
from jax import lax
from jax.experimental import pallas as pl
from jax.experimental.pallas import tpu as pltpu

F32, BF16 = jnp.float32, jnp.bfloat16
N_HEADS = 16
HEAD_DIM = 64
LANES = 128
MOBA_BLOCK = 256
MOBA_TOPK = 3
PAGE_SIZE = 128
RMS_EPS = 1e-6
LNX_EPS = 64e-5
NEG = -1e30
VMEM_LIMIT_BYTES = 56 * 1024 * 1024
SCAN_CHUNK = 64
FFN_TILE_MAX = 4096
POS_LANES = 3
UNROLL = 4
VT_ROWS = 80


def _cparams(*sem):
    return pltpu.CompilerParams(dimension_semantics=sem, vmem_limit_bytes=VMEM_LIMIT_BYTES)


def _rms(x, g):
    return x * lax.rsqrt(jnp.mean(x * x, axis=-1, keepdims=True) + RMS_EPS) * g


def _sigmoid(x):
    return 1.0 / (1.0 + jnp.exp(-x))


_NN = (((1,), (0,)), ((), ()))
_NT = (((1,), (1,)), ((), ()))


def _dot(a, b, dims=_NN):
    return lax.dot_general(a, b, dims, preferred_element_type=F32)


def _bf(x):
    return x.astype(BF16)


def _bdot(a, w):
    return _dot(a.astype(BF16), w)


def _split2(x):
    hi = x.astype(BF16)
    lo = (x - hi.astype(F32)).astype(BF16)
    return hi, lo


def _dot3(a, b, dims=_NN):
    ah, al = _split2(a)
    bh, bl = _split2(b)
    return _dot(ah, bh, dims) + (_dot(ah, bl, dims) + _dot(al, bh, dims))


def _full(shape):
    n = len(shape)
    return pl.BlockSpec(shape, lambda *_: (0,) * n)


def _rms_kernel(x_ref, g_ref, o_ref):
    o_ref[...] = _rms(x_ref[...], g_ref[...])


def rmsnorm_call(x, g, tm):
    t, d = x.shape
    row = pl.BlockSpec((tm, d), lambda i: (i, 0))
    return pl.pallas_call(
        _rms_kernel, grid=(t // tm,), in_specs=[row, _full((1, d))], out_specs=row,
        out_shape=jax.ShapeDtypeStruct((t, d), F32), compiler_params=_cparams("parallel"),
        name="rmsnorm")(x, g)


def _rwkv_proj_kernel(x_ref, xp_ref, mu_ref, wr, wk, wv, w1, w2, a1, a2, g1, g2, w0, a0, *rest, tiles_per_seq):
    if tiles_per_seq:
        gn_ref, sh_ref, r_o, k_o, v_o, lw_o, a_o, g_o, last_o = rest
        hn = _rms(x_ref[...], gn_ref[...])
        last_o[...] = hn[hn.shape[0] - 8:, :]
        prev_last = _rms(xp_ref[...], gn_ref[...])[7:8, :]
        seq_start = pl.program_id(0) % tiles_per_seq == 0
        carry = jnp.where(seq_start, sh_ref[...], prev_last)
        rowi = lax.broadcasted_iota(jnp.int32, hn.shape, 0)
        xp = jnp.where(rowi == 0, carry, pltpu.roll(hn, 1, 0))
    else:
        r_o, k_o, v_o, lw_o, a_o, g_o = rest
        hn = x_ref[...]
        xp = xp_ref[...]
    xx = xp - hn

    def mix(i):
        return hn + xx * mu_ref[i:i + 1, :]

    r_o[...] = _bdot(mix(0), wr[...])
    k_o[...] = _bdot(mix(2), wk[...])
    v_o[...] = _bdot(mix(3), wv[...])
    u = w0[...] + _bdot(jnp.tanh(_bdot(mix(1), w1[...])), w2[...])
    lw_o[...] = -math.exp(-0.5) * _sigmoid(u)
    a_o[...] = _sigmoid(a0[...] + _bdot(_bdot(mix(4), a1[...]), a2[...]))
    g_o[...] = _bdot(_sigmoid(_bdot(mix(5), g1[...])), g2[...])


def rwkv_proj_call(x, xp, mu, wr, wk, wv, w1, w2, a1, a2, g1, g2, w0, a0, tm, norm=None):
    t, d = x.shape
    row = pl.BlockSpec((tm, d), lambda i: (i, 0))
    ws = [mu, wr, wk, wv, w1, w2, a1, a2, g1, g2, w0, a0]
    outs = [jax.ShapeDtypeStruct((t, d), F32)] * 6
    ospecs = [row] * 6
    if norm is None:
        ins, specs, tps = [x, xp], [row, row], 0
    else:
        gain, shift = norm
        nseq = shift.shape[0]
        tps = t // tm // nseq
        ins = [x, x]
        specs = [row, pl.BlockSpec((8, d), lambda i: (jnp.maximum(i * (tm // 8) - 1, 0), 0))]
        ws = ws + [gain, shift]
        outs = outs + [jax.ShapeDtypeStruct((nseq, 8, d), F32)]
        ospecs = ospecs + [pl.BlockSpec((None, 8, d), lambda i: (i // tps, 0, 0))]
    wspecs = [_full(w.shape) for w in ws]
    if norm is not None:
        wspecs[-1] = pl.BlockSpec((None, 1, d), lambda i: (i // tps, 0, 0))
    return pl.pallas_call(
        functools.partial(_rwkv_proj_kernel, tiles_per_seq=tps), grid=(t // tm,),
        in_specs=specs + wspecs, out_specs=ospecs, out_shape=outs,
        compiler_params=_cparams("arbitrary"), name="rwkv_proj")(*ins, *ws)


def _post_mixer_kernel(x_ref, wo_ref, h_ref, gf_ref, wg_ref, wu_ref, wd_ref, p_ref, gp_ref, pg_ref, pp_ref,
                       *rest, final):
    if final:
        gfin_ref, o_ref, xn_scr = rest
    else:
        o_ref, xn_scr = rest
    f = pl.program_id(1)

    @pl.when(f == 0)
    def _():
        h1 = h_ref[...] + _bdot(x_ref[...], wo_ref[...])
        xn_scr[...] = _rms(h1, gf_ref[...]).astype(BF16)
        o_ref[...] = h1

    xn = xn_scr[...]
    gt = _dot(xn, wg_ref[...])
    up = _dot(xn, wu_ref[...])
    o_ref[...] += _bdot(gt * _sigmoid(gt) * up, wd_ref[...])

    @pl.when(f == pl.num_programs(1) - 1)
    def _():
        h2 = o_ref[...]
        gate = _sigmoid(_bdot(_rms(h2, gp_ref[...]), pg_ref[...]))
        h3 = h2 + _bdot(p_ref[...], pp_ref[...]) * gate
        o_ref[...] = _rms(h3, gfin_ref[...]) if final else h3


def post_mixer_call(x, wo, h, g_ffn, wg, wu, wd, p, g_ple, ple_wg, ple_wp, tm, tf, g_final=None):
    t, d = h.shape
    f = wg.shape[1]
    row = pl.BlockSpec((tm, d), lambda i, j: (i, 0))
    const = lambda shape: pl.BlockSpec(shape, lambda i, j: (0,) * len(shape), pipeline_mode=pl.Buffered(1))
    ins = [x, wo, h, g_ffn, wg, wu, wd, p, g_ple, ple_wg, ple_wp]
    specs = [row, const(wo.shape), row, const((1, d)),
             pl.BlockSpec((d, tf), lambda i, j: (0, j), pipeline_mode=pl.Buffered(1)), pl.BlockSpec((d, tf), lambda i, j: (0, j), pipeline_mode=pl.Buffered(1)),
             pl.BlockSpec((tf, d), lambda i, j: (j, 0), pipeline_mode=pl.Buffered(1)),
             pl.BlockSpec((tm, p.shape[1]), lambda i, j: (i, 0)), const((1, d)),
             const(ple_wg.shape), const(ple_wp.shape)]
    if g_final is not None:
        ins.append(g_final)
        specs.append(const((1, d)))
    return pl.pallas_call(
        functools.partial(_post_mixer_kernel, final=g_final is not None), grid=(t // tm, f // tf),
        in_specs=specs, out_specs=row, out_shape=jax.ShapeDtypeStruct((t, d), F32),
        scratch_shapes=[pltpu.VMEM((tm, d), BF16)],
        compiler_params=_cparams("parallel", "arbitrary"), name="post_mixer")(*ins)


def _kvq_kernel(h_ref, gkv_ref, gq_ref, wk_ref, wv_ref, wq_ref, *rest, prompt):
    h = h_ref[...]
    hk = _rms(h, gkv_ref[...]).astype(BF16)
    k = _dot(hk, wk_ref[...])
    if not prompt:
        k_o, v_o, q_o = rest
        k_o[...] = k
        v_o[...] = _dot(hk, wv_ref[...])
        q_o[...] = _bdot(_rms(h, gq_ref[...]), wq_ref[...])
        return
    wkt_ref, kt_o, vt_o, q_o, kb_o, vtb_o, km_o = rest
    q_o[...] = _bdot(_rms(h, gq_ref[...]), wq_ref[...])
    kt_o[...] = _dot(wkt_ref[...], hk, _NT)
    vt = _dot(wv_ref[...], hk, _NT)
    vt_o[...] = vt
    tm, d = k.shape
    lane = lax.broadcasted_iota(jnp.int32, (tm, LANES), 1)
    rpos = lax.broadcasted_iota(jnp.int32, (tm, LANES), 0).astype(F32)
    for p in range(d // LANES):
        kp = k[:, p * LANES:(p + 1) * LANES]
        for hh in range(2):
            own = (lane < HEAD_DIM) == (hh == 0)
            pos = lane - (HEAD_DIM if hh == 0 else 0)
            extra = jnp.where(jnp.logical_and(pos >= 0, pos < POS_LANES), rpos, 0.0)
            c0 = (2 * p + hh) * LANES
            kb_o[:, c0:c0 + LANES] = jnp.where(own, kp, extra).astype(BF16)
    ones_rows = jnp.where(lax.broadcasted_iota(jnp.int32, (VT_ROWS - HEAD_DIM, tm), 0) == 0, 1.0, 0.0)
    for hd in range(d // HEAD_DIM):
        vtb_o[hd * VT_ROWS:hd * VT_ROWS + HEAD_DIM, :] = vt[hd * HEAD_DIM:(hd + 1) * HEAD_DIM, :].astype(BF16)
        vtb_o[hd * VT_ROWS + HEAD_DIM:(hd + 1) * VT_ROWS, :] = ones_rows.astype(BF16)
    km_o[...] = jnp.sum(k, axis=0, keepdims=True) * (1.0 / MOBA_BLOCK)


def kvq_call(h, gkv, gq, wk, wv, wq, tm, wkt=None, seq=None):
    t, d = h.shape
    prompt = wkt is not None
    row = pl.BlockSpec((tm, d), lambda i: (i, 0))
    ins = [h, gkv, gq, wk, wv, wq]
    specs = [row, _full(gkv.shape), _full(gq.shape), _full(wk.shape), _full(wv.shape), _full(wq.shape)]
    if prompt:
        assert tm == MOBA_BLOCK
        tps = seq // tm
        nh = d // HEAD_DIM
        tspec = lambda rows: pl.BlockSpec((None, rows, tm), lambda i: (i // tps, 0, i % tps))
        ins.append(wkt)
        specs.append(_full(wkt.shape))
        outs = [jax.ShapeDtypeStruct((t // seq, d, seq), F32)] * 2 + [
            jax.ShapeDtypeStruct((t, d), F32),
            jax.ShapeDtypeStruct((t, nh * LANES), BF16),
            jax.ShapeDtypeStruct((t // seq, nh * VT_ROWS, seq), BF16),
            jax.ShapeDtypeStruct((t // tm, 1, d), F32)]
        ospecs = [tspec(d), tspec(d), row, pl.BlockSpec((tm, nh * LANES), lambda i: (i, 0)),
                  tspec(nh * VT_ROWS), pl.BlockSpec((None, 1, d), lambda i: (i, 0, 0))]
    else:
        outs = [jax.ShapeDtypeStruct((t, d), F32)] * 3
        ospecs = [row] * 3
    return pl.pallas_call(
        functools.partial(_kvq_kernel, prompt=prompt), grid=(t // tm,), in_specs=specs,
        out_specs=ospecs, out_shape=outs, compiler_params=_cparams("parallel"), name="kvq")(*ins)


def _scan_kernel(r_ref, k_ref, v_ref, lw_ref, a_ref, g_ref, kk_ref, ka_ref, rk_ref, lnw_ref, lnb_ref,
                 s0_ref, o_ref, sout_ref, z_scr, *, C, NP, NCH):
    c = pl.program_id(2)
    pairs = range(NP)

    def split3(x):
        x1 = x.astype(BF16)
        r1 = x - x1.astype(F32)
        x2 = r1.astype(BF16)
        return x1, x2, (r1 - x2.astype(F32)).astype(BF16)

    sel_r = lax.broadcasted_iota(jnp.int32, (LANES, HEAD_DIM), 0)
    sel_c = lax.broadcasted_iota(jnp.int32, (LANES, HEAD_DIM), 1)
    place = [jnp.where(sel_r == sel_c + h * HEAD_DIM, 1.0, 0.0).astype(BF16) for h in range(2)]
    pk_r = lax.broadcasted_iota(jnp.int32, (HEAD_DIM, LANES), 0)
    pk_c = lax.broadcasted_iota(jnp.int32, (HEAD_DIM, LANES), 1)
    pick = [jnp.where(pk_c == pk_r + h * HEAD_DIM, 1.0, 0.0).astype(BF16) for h in range(2)]
    zero_s = jnp.zeros((HEAD_DIM, HEAD_DIM), BF16)

    @pl.when(c == 0)
    def _():
        for p in pairs:
            z = jnp.zeros((LANES, LANES), F32)
            for h in range(2):
                for part in split3(s0_ref[2 * p + h]):
                    rows = [part, zero_s] if h == 0 else [zero_s, part]
                    z = z + _dot(place[h], jnp.concatenate(rows, axis=0), _NT)
            z_scr[p] = z

    lane = lax.broadcasted_iota(jnp.int32, (1, LANES), 1)
    m0 = lane < HEAD_DIM
    masks = (m0, jnp.logical_not(m0))
    units = [(ch, p) for ch in range(NCH) for p in pairs]
    heads = [(u, h) for u in range(len(units)) for h in range(2)]

    def hsum(x):
        s0 = jnp.sum(jnp.where(m0, x, 0.0), axis=-1, keepdims=True)
        s1 = jnp.sum(jnp.where(m0, 0.0, x), axis=-1, keepdims=True)
        return jnp.where(m0, s0, s1)

    def tile(ref, u):
        ch, p = units[u]
        return ref[ch * C:(ch + 1) * C, p * LANES:(p + 1) * LANES]

    def ptile(ref, u):
        p = units[u][1]
        return ref[:, p * LANES:(p + 1) * LANES]

    ri = lax.broadcasted_iota(jnp.int32, (C, C), 0)
    ci = lax.broadcasted_iota(jnp.int32, (C, C), 1)
    ltri = jnp.where(ci <= ri, 1.0, 0.0).astype(BF16)
    strict_c = ci < ri
    incl_c = ci <= ri
    rw = lax.broadcasted_iota(jnp.int32, (C, 2 * C), 0)
    cw = lax.broadcasted_iota(jnp.int32, (C, 2 * C), 1)
    strict_w = jnp.where(cw < C, cw, cw - C) < rw
    right_w = cw >= C
    eye_w = cw - C == rw
    zero_c = jnp.zeros((C, LANES), BF16)
    rz = lax.broadcasted_iota(jnp.int32, (LANES, LANES), 0)
    cz = lax.broadcasted_iota(jnp.int32, (LANES, LANES), 1)
    same_head = (rz < HEAD_DIM) == (cz < HEAD_DIM)

    nu = range(len(units))

    r = [tile(r_ref, u) for u in nu]
    k = [tile(k_ref, u) for u in nu]
    v = [tile(v_ref, u) for u in nu]
    lw = [tile(lw_ref, u) for u in nu]
    a = [tile(a_ref, u) for u in nu]

    def cumsum(x):
        l1 = x.astype(BF16)
        rem = x - l1.astype(F32)
        l2 = rem.astype(BF16)
        l3 = (rem - l2.astype(F32)).astype(BF16)
        return _dot(ltri, l1) + (_dot(ltri, l2) + _dot(ltri, l3))

    cl = [cumsum(lw[u]) for u in nu]

    kmod, ars, rhs_cols, bk_t, gc_col, vs = [], [], [], [], [], []
    for u in nu:
        kkr = k[u] * ptile(kk_ref, u)
        kk = kkr / jnp.maximum(jnp.sqrt(hsum(kkr * kkr)), 1e-12)
        beta = kk * a[u]
        km = k[u] * (1.0 + (a[u] - 1.0) * ptile(ka_ref, u))
        kmod.append(km)
        igam = jnp.exp(-cl[u])
        a_t = -kk * jnp.exp(cl[u] - lw[u])
        r_t = r[u] * jnp.exp(cl[u])
        b_h = beta * igam
        k_h = km * igam
        gout = jnp.exp(cl[u][C - 1:C, :] - cl[u])
        ars.append(_bf(jnp.concatenate([a_t, r_t], axis=0)))
        rhs_cols.append(_bf(jnp.concatenate([b_h, b_h, k_h], axis=0)))
        bk_t.append(_bf(jnp.concatenate([(beta * gout).T, (km * gout).T], axis=1)))
        gc_col.append(jnp.exp(jnp.sum(lw[u].T, axis=1, keepdims=True)))
        vs.append(_bf(v[u]))

    gfull = [_dot(jnp.where(masks[h], ars[u], jnp.zeros_like(ars[u])), rhs_cols[u], _NT)
             for u, h in heads]
    w = [jnp.where(strict_w, g[:C, :2 * C], 0.0) for g in gfull]
    gk = [_bf(jnp.concatenate([jnp.where(strict_c, g[:C, 2 * C:], 0.0),
                               jnp.where(incl_c, g[C:, 2 * C:], 0.0)], axis=0)) for g in gfull]
    arb = [_bf(jnp.where(incl_c, g[C:, :C], 0.0)) for g in gfull]
    av = [_dot(gk[i], vs[u]) for i, (u, h) in enumerate(heads)]
    avs = [jnp.where(m0, av[2 * u], av[2 * u + 1]) for u in nu]

    for _ in range(int(math.log2(C))):
        wb = [_bf(x) for x in w]
        w = [jnp.where(right_w, w[i], 0.0) + _dot(wb[i][:, :C], wb[i]) for i in range(len(heads))]
    minv = [_bf(jnp.where(right_w, x, 0.0) + jnp.where(eye_w, 1.0, 0.0)) for x in w]

    z = [z_scr[p] for p in pairs]
    inv_n = 1.0 / HEAD_DIM
    for ch in range(NCH):
        us = [ch * NP + p for p in pairs]
        xs = [_dot(ars[u], _bf(z[p])) for p, u in zip(pairs, us)]
        rhs = [jnp.concatenate([zero_c, _bf(x[:C] + avs[u][:C])], axis=0) for x, u in zip(xs, us)]
        uh = [_dot(minv[2 * u + h], rhs[p]) for p, u in zip(pairs, us) for h in range(2)]
        ub = [_bf(jnp.where(m0, uh[2 * p], uh[2 * p + 1])) for p in pairs]
        oh = [_dot(arb[2 * u + h], ub[p]) for p, u in zip(pairs, us) for h in range(2)]
        zd = [_dot(bk_t[u], jnp.concatenate([ub[p], vs[u]], axis=0)) for p, u in zip(pairs, us)]
        z = [jnp.where(same_head, z[p] * gc_col[u] + zd[p], 0.0) for p, u in zip(pairs, us)]
        for p, u in zip(pairs, us):
            o = xs[p][C:] + avs[u][C:] + jnp.where(m0, oh[2 * p], oh[2 * p + 1])
            mean = hsum(o) * inv_n
            dlt = o - mean
            var = hsum(dlt * dlt) * inv_n
            on = dlt * lax.rsqrt(var + LNX_EPS) * ptile(lnw_ref, u) + ptile(lnb_ref, u)
            bonus = hsum(r[u] * kmod[u] * ptile(rk_ref, u)) * v[u]
            o_ref[ch * C:(ch + 1) * C, p * LANES:(p + 1) * LANES] = (on + bonus) * tile(g_ref, u)

    for p in pairs:
        z_scr[p] = z[p]

        @pl.when(c == pl.num_programs(2) - 1)
        def _():
            for h in range(2):
                out = jnp.zeros((HEAD_DIM, HEAD_DIM), F32)
                for part in split3(z[p][h * HEAD_DIM:(h + 1) * HEAD_DIM, :]):
                    out = out + _dot(pick[h], part, _NT)
                sout_ref[2 * p + h] = out


def scan_call(r, k, v, lw, a, g, k_k, k_a, r_k, lnw, lnb, s0, nb, seq, C, NP, NCH):
    t, d = r.shape
    npairs = d // LANES
    rows = C * NCH
    nc = seq // rows
    w = NP * LANES
    row = pl.BlockSpec((rows, w), lambda b, p, c: (b * nc + c, p))
    par = pl.BlockSpec((1, w), lambda b, p, c: (0, p))
    sspec = pl.BlockSpec((None, 2 * NP, HEAD_DIM, HEAD_DIM), lambda b, p, c: (b, p, 0, 0))
    return pl.pallas_call(
        functools.partial(_scan_kernel, C=C, NP=NP, NCH=NCH), grid=(nb, npairs // NP, nc),
        in_specs=[row] * 6 + [par] * 5 + [sspec],
        out_specs=[row, sspec],
        out_shape=[jax.ShapeDtypeStruct((t, d), F32),
                   jax.ShapeDtypeStruct((nb, 2 * npairs, HEAD_DIM, HEAD_DIM), F32)],
        scratch_shapes=[pltpu.VMEM((NP, LANES, LANES), F32)],
        compiler_params=_cparams("parallel", "parallel", "arbitrary"), name="rwkv_scan")(
            r, k, v, lw, a, g, k_k, k_a, r_k, lnw, lnb, s0)


def _top3_rows(gate, rowi, nrow):
    sel = jnp.zeros(gate.shape, jnp.bool_)
    for _ in range(MOBA_TOPK):
        mx = jnp.max(gate, axis=0, keepdims=True)
        idx = jnp.min(jnp.where(gate == mx, rowi, nrow), axis=0, keepdims=True)
        pick = jnp.logical_and(rowi == idx, mx > -jnp.inf)
        sel = jnp.logical_or(sel, pick)
        gate = jnp.where(pick, -jnp.inf, gate)
    return sel


def _moba_prompt_kernel(sl_ref, q_ref, k_ref, vt_ref, km_ref, o_ref, sel_scr, sa_scr, sb_scr, pa_scr, pb_scr,
                        *, nb):
    p = pl.program_id(1)
    i = pl.program_id(2)
    blk = MOBA_BLOCK
    lane = lax.broadcasted_iota(jnp.int32, (1, LANES), 1)
    m0 = lane < HEAD_DIM
    masks = (m0, jnp.logical_not(m0))
    q = q_ref[...]
    km = km_ref[...]
    rowi = lax.broadcasted_iota(jnp.int32, (nb, blk), 0)
    causal = (lax.broadcasted_iota(jnp.int32, (blk, blk), 1)
              >= lax.broadcasted_iota(jnp.int32, (blk, blk), 0))
    log2e = 1.0 / math.log(2.0)
    scale2 = log2e / math.sqrt(HEAD_DIM)

    qb, slopes = [], []
    for h in range(2):
        qh = jnp.where(masks[h], q, 0.0)
        gate = jnp.where(rowi < i, _dot3(km, qh, _NT), -jnp.inf)
        sel_scr[h] = jnp.where(_top3_rows(gate, rowi, nb), 0.0, NEG)
        slope = sl_ref[2 * p + h] * log2e
        slopes.append(slope)
        pos = lane - (HEAD_DIM if h == 0 else 0)
        rest = jnp.full((1, LANES), slope, F32)
        terms = jnp.zeros((1, LANES), F32)
        for t in range(POS_LANES):
            part = rest.astype(BF16).astype(F32)
            terms = jnp.where(pos == t, part, terms)
            rest = rest - part
        qb.append(jnp.where(masks[h], q * scale2, terms).astype(BF16))

    def block_off(jb):
        return pl.multiple_of(jnp.minimum(jb, nb - 1) * blk, blk)

    def scores_into(dst, jb):
        kt = k_ref[pl.ds(block_off(jb), blk), :]
        for h in range(2):
            dst[h] = _dot(kt[:, h * LANES:(h + 1) * LANES], qb[h], _NT)

    def pv_of(p_src, jb):
        vt = vt_ref[:, pl.ds(block_off(jb), blk)]
        return [_dot(vt[h * VT_ROWS:(h + 1) * VT_ROWS, :], p_src[h]) for h in range(2)]

    def step(s_src, p_dst, rowvec, pend, carry, own):
        pv = None if own else pv_of(*pend)
        new = []
        for h in range(2):
            m, a_prev, acc = carry[h]
            s = s_src[h]
            if own:
                s = jnp.where(causal, s, NEG)
            m_new = jnp.maximum(m, jnp.max(s, axis=0, keepdims=True) + rowvec[h])
            p_dst[h] = jnp.exp2(s - (m_new - rowvec[h])).astype(BF16)
            new.append((m_new, jnp.exp2(m - m_new), acc if own else a_prev * acc + pv[h]))
        return tuple(new)

    def past_rowvec(jb):
        gap = ((i - jb) * blk).astype(F32)
        jc = jnp.minimum(jb, nb - 1)
        return [jnp.where(jb < i, sel_scr[h, pl.ds(jc, 1), :] - slopes[h] * gap, NEG) for h in range(2)]

    scores_into(sb_scr, i)
    scores_into(sa_scr, 0)
    zero_row = jnp.zeros((1, blk), F32)
    init = tuple((jnp.full((1, blk), NEG, F32), zero_row, jnp.zeros((VT_ROWS, blk), F32)) for _ in range(2))
    carry = step(sb_scr, pb_scr, (zero_row, zero_row), None, init, True)

    def trip(unroll, base):
        def body(jj, carry):
            j0 = base + unroll * jj
            for u in range(0, unroll, 2):
                scores_into(sb_scr, j0 + u + 1)
                prev = jnp.where(j0 == 0, i, j0 - 1) if u == 0 else j0 + u - 1
                carry = step(sa_scr, pa_scr, past_rowvec(j0 + u), (pb_scr, prev), carry, False)
                scores_into(sa_scr, j0 + u + 2)
                carry = step(sb_scr, pb_scr, past_rowvec(j0 + u + 1), (pa_scr, j0 + u), carry, False)
            return carry
        return body

    full = i // UNROLL
    done = full * UNROLL
    rest = (i - done + 1) // 2
    carry = lax.fori_loop(0, full, trip(UNROLL, 0), carry)
    carry = lax.fori_loop(0, rest, trip(2, done), carry)
    covered = done + 2 * rest
    pv = pv_of(pb_scr, jnp.where(covered == 0, i, covered - 1))
    acc = [carry[h][1] * carry[h][2] + pv[h] for h in range(2)]
    out_t = jnp.concatenate([acc[h][:HEAD_DIM] / acc[h][HEAD_DIM:HEAD_DIM + 1] for h in range(2)], axis=0)
    o_ref[...] = out_t.T


def moba_prompt_call(slopes, q, kb, vt, kmean, nbatch, seq):
    t, d = q.shape
    nb = seq // MOBA_BLOCK
    npairs = d // LANES
    qspec = pl.BlockSpec((MOBA_BLOCK, LANES), lambda b, p, i: (b * nb + i, p))
    tile = pltpu.VMEM((2, MOBA_BLOCK, MOBA_BLOCK), F32)
    ptile = pltpu.VMEM((2, MOBA_BLOCK, MOBA_BLOCK), BF16)
    return pl.pallas_call(
        functools.partial(_moba_prompt_kernel, nb=nb), grid=(nbatch, npairs, nb),
        in_specs=[pl.BlockSpec(memory_space=pltpu.SMEM), qspec,
                  pl.BlockSpec((seq, 2 * LANES), lambda b, p, i: (b, p)),
                  pl.BlockSpec((None, 2 * VT_ROWS, seq), lambda b, p, i: (b, p, 0)),
                  pl.BlockSpec((None, nb, LANES), lambda b, p, i: (b, 0, p))],
        out_specs=qspec, out_shape=jax.ShapeDtypeStruct((t, d), F32),
        scratch_shapes=[pltpu.VMEM((2, nb, MOBA_BLOCK), F32), tile, tile, ptile, ptile],
        compiler_params=_cparams("parallel", "parallel", "arbitrary"), name="moba_prompt")(
            slopes, q, kb, vt, kmean)


def _head_tiles(page_refs, h):
    return jnp.concatenate([ref[h] for ref in page_refs], axis=1).astype(BF16)


def _sample_scores_kernel(pt_ref, q_ref, *refs, bps):
    del pt_ref
    page_refs, (sc_ref, g_ref) = refs[:2 * bps], refs[2 * bps:]
    j = pl.program_id(1)
    rows = q_ref.shape[1]
    lane = lax.broadcasted_iota(jnp.int32, g_ref.shape[1:], 1)

    @pl.when(j == 0)
    def _():
        g_ref[...] = jnp.zeros(g_ref.shape, F32)

    for h in range(N_HEADS):
        q = q_ref[h]
        q1 = q.astype(BF16)
        r1 = q - q1.astype(F32)
        q2 = r1.astype(BF16)
        q3 = (r1 - q2.astype(F32)).astype(BF16)
        s3 = _dot(jnp.concatenate([q1, q2, q3], axis=0), _head_tiles(page_refs, h))
        sc_ref[h] = s3[:rows]
        full = s3[:rows] + (s3[rows:2 * rows] + s3[2 * rows:])
        g = g_ref[h]
        for s in range(bps):
            gcol = jnp.sum(full[:, s * MOBA_BLOCK:(s + 1) * MOBA_BLOCK], axis=1, keepdims=True)
            g = jnp.where(lane == j * bps + s, gcol * (1.0 / MOBA_BLOCK), g)
        g_ref[h] = g


def _paged_specs(page_table, page_shape, bps):
    npg = page_table.shape[1]
    ppb = MOBA_BLOCK // PAGE_SIZE
    assert ppb == 2
    return [pl.BlockSpec((None,) + page_shape,
                         functools.partial(lambda b, j, pt, o: (pt[b * npg + ppb * bps * j + o], 0, 0, 0), o=o))
            for o in range(ppb * bps)]


def _blocks_per_step(nbp):
    return max(n for n in (8, 4, 2, 1) if nbp % n == 0)


def sample_scores_call(page_table, q4, cache_kt):
    bs, nh, rows, dh = q4.shape
    nbp = page_table.shape[1] * PAGE_SIZE // MOBA_BLOCK
    bps = _blocks_per_step(nbp)
    per_b = lambda last: pl.BlockSpec((None, nh, rows, last), lambda b, j, pt: (b, 0, 0, 0))
    gs = pltpu.PrefetchScalarGridSpec(
        num_scalar_prefetch=1, grid=(bs, nbp // bps),
        in_specs=[per_b(dh)] + _paged_specs(page_table, (nh, dh, PAGE_SIZE), bps),
        out_specs=[pl.BlockSpec((None, nh, rows, bps * MOBA_BLOCK), lambda b, j, pt: (b, 0, 0, j)), per_b(nbp)])
    return pl.pallas_call(
        functools.partial(_sample_scores_kernel, bps=bps), grid_spec=gs,
        out_shape=[jax.ShapeDtypeStruct((bs, nh, rows, nbp * MOBA_BLOCK), F32),
                   jax.ShapeDtypeStruct((bs, nh, rows, nbp), F32)],
        compiler_params=_cparams("parallel", "arbitrary"), name="sample_scores")(
            page_table.reshape(-1), q4, *([cache_kt] * (2 * bps)))


def _sample_attend_kernel(pt_ref, sl_ref, sc_ref, g_ref, q_ref, ko_ref, vo_ref, *refs, nbp, bps, past_len):
    del pt_ref
    page_refs, (o_ref, m_scr, l_scr, acc_scr, sel_scr) = refs[:2 * bps], refs[2 * bps:]
    j = pl.program_id(1)
    rows = q_ref.shape[1]
    scale = 1.0 / math.sqrt(HEAD_DIM)
    tq = lax.broadcasted_iota(jnp.int32, (rows, 1), 0)
    heads = range(N_HEADS)

    @pl.when(j == 0)
    def _():
        lane = lax.broadcasted_iota(jnp.int32, (rows, nbp), 1)
        tk = lax.broadcasted_iota(jnp.int32, (rows, rows), 1)
        for h in heads:
            gate = g_ref[h]
            sel = jnp.zeros(gate.shape, jnp.bool_)
            for _ in range(MOBA_TOPK):
                mx = jnp.max(gate, axis=1, keepdims=True)
                idx = jnp.min(jnp.where(gate == mx, lane, nbp), axis=1, keepdims=True)
                pick = jnp.logical_and(lane == idx, mx > -jnp.inf)
                sel = jnp.logical_or(sel, pick)
                gate = jnp.where(pick, -jnp.inf, gate)
            sel_scr[h] = jnp.where(sel, 0.0, NEG)
            s = _dot(q_ref[h].astype(BF16), ko_ref[h].astype(BF16), _NT) * scale
            s = jnp.where(tk <= tq, s - sl_ref[h] * (tq - tk).astype(F32), NEG)
            m = jnp.max(s, axis=1, keepdims=True)
            pr = jnp.exp(s - m)
            m_scr[h] = m
            l_scr[h] = jnp.sum(pr, axis=1, keepdims=True)
            acc_scr[h] = _dot(pr.astype(BF16), vo_ref[h].astype(BF16))

    width = bps * MOBA_BLOCK
    lane_k = lax.broadcasted_iota(jnp.int32, (rows, width), 1)
    lane_b = lax.broadcasted_iota(jnp.int32, (rows, nbp), 1)
    dist = ((past_len + tq) - (j * width + lane_k)).astype(F32)
    prs, alphas = [], []
    for h in heads:
        sel = sel_scr[h]
        bias = jnp.sum(jnp.where(lane_b == j * bps, sel, 0.0), axis=1, keepdims=True)
        for sb in range(1, bps):
            bias_sb = jnp.sum(jnp.where(lane_b == j * bps + sb, sel, 0.0), axis=1, keepdims=True)
            bias = jnp.where(lane_k >= sb * MOBA_BLOCK, bias_sb, bias)
        s = sc_ref[h] * scale - sl_ref[h] * dist + bias
        m_old = m_scr[h]
        m_new = jnp.maximum(m_old, jnp.max(s, axis=1, keepdims=True))
        alpha = jnp.exp(m_old - m_new)
        pr = jnp.exp(s - m_new)
        m_scr[h] = m_new
        l_scr[h] = alpha * l_scr[h] + jnp.sum(pr, axis=1, keepdims=True)
        prs.append(pr.astype(BF16))
        alphas.append(alpha)
    for h in heads:
        acc_scr[h] = alphas[h] * acc_scr[h] + _dot(prs[h], _head_tiles(page_refs, h), _NT)

    @pl.when(j == pl.num_programs(1) - 1)
    def _():
        for h in heads:
            o_ref[h] = acc_scr[h] / l_scr[h]


def sample_attend_call(page_table, slopes, scores, gate, q4, k_own, v_own, cache_vt):
    bs, nh, rows, dh = q4.shape
    npg = page_table.shape[1]
    nbp = npg * PAGE_SIZE // MOBA_BLOCK
    bps = _blocks_per_step(nbp)
    per_b = lambda last: pl.BlockSpec((None, nh, rows, last), lambda b, j, pt: (b, 0, 0, 0))
    gs = pltpu.PrefetchScalarGridSpec(
        num_scalar_prefetch=1, grid=(bs, nbp // bps),
        in_specs=[pl.BlockSpec(memory_space=pltpu.SMEM),
                  pl.BlockSpec((None, nh, rows, bps * MOBA_BLOCK), lambda b, j, pt: (b, 0, 0, j)),
                  per_b(nbp), per_b(dh), per_b(dh), per_b(dh)]
                 + _paged_specs(page_table, (nh, dh, PAGE_SIZE), bps),
        out_specs=per_b(dh),
        scratch_shapes=[pltpu.VMEM((nh, rows, 1), F32), pltpu.VMEM((nh, rows, 1), F32),
                        pltpu.VMEM((nh, rows, dh), F32), pltpu.VMEM((nh, rows, nbp), F32)])
    return pl.pallas_call(
        functools.partial(_sample_attend_kernel, nbp=nbp, bps=bps, past_len=npg * PAGE_SIZE),
        grid_spec=gs, out_shape=jax.ShapeDtypeStruct((bs, nh, rows, dh), F32),
        compiler_params=_cparams("parallel", "arbitrary"), name="sample_attend")(
            page_table.reshape(-1), slopes, scores, gate, q4, k_own, v_own, *([cache_vt] * (2 * bps)))


def _row_tile(t):
    for tm in (512, 256, 128):
        if t % tm == 0:
            return tm
    raise ValueError(f"token count {t} is not a multiple of 128")


def _ffn_tile(f):
    for nf in (1, 2, 4, 11, 22):
        if f % nf == 0 and (f // nf) % LANES == 0 and f // nf <= FFN_TILE_MAX:
            return f // nf
    raise ValueError(f"unsupported FFN width {f}")


def _trunk(x, p, wkv_in, shift_in, W, cache=None):
    b, t, d = x.shape
    nt = b * t
    tm = _row_tile(nt)
    h = x.reshape(nt, d)
    row = lambda v: v.reshape(1, -1)

    proj_w = (W['rwkv_mu'][0], W['w_r'], W['w_k'], W['w_v'], W['rwkv_w1'], W['rwkv_w2'],
              W['rwkv_a1'], W['rwkv_a2'], W['rwkv_g1'], W['rwkv_g2'], row(W['rwkv_w0'][0]), row(W['rwkv_a0'][0]))
    if t % tm == 0:
        r, k, v, lw, a, g, last = rwkv_proj_call(h, None, *proj_w, tm,
                                                 norm=(row(W['norm_mix'][0]), shift_in[:, None, :]))
        shift_out = last[:, -1]
    else:
        hn = rmsnorm_call(h, row(W['norm_mix'][0]), tm)
        hn3 = hn.reshape(b, t, d)
        xp = jnp.concatenate([shift_in[:, None, :], hn3[:, :-1]], axis=1).reshape(nt, d)
        shift_out = hn3[:, -1]
        r, k, v, lw, a, g = rwkv_proj_call(hn, xp, *proj_w, tm)
    chunk = SCAN_CHUNK if t % SCAN_CHUNK == 0 else 8
    tp = -(-t // chunk) * chunk
    seqs = (r, k, v, lw, a, g)
    if tp != t:
        seqs = tuple(jnp.pad(s.reshape(b, t, d), ((0, 0), (0, tp - t), (0, 0))).reshape(b * tp, d)
                     for s in seqs)
    o, wkv_out = scan_call(*seqs, row(W['rwkv_k_k'][0]), row(W['rwkv_k_a'][0]), row(W['rwkv_r_k'][0]),
                      row(W['rwkv_lnx_w'][0]), row(W['rwkv_lnx_b'][0]), wkv_in, b, tp, chunk,
                      d // LANES, 2 if tp % (2 * chunk) == 0 else 1)
    if tp != t:
        o = o.reshape(b, tp, d)[:, :t].reshape(nt, d)
    tf = _ffn_tile(W['ffn_w_gate'][0].shape[1])

    def post_mixer(x_mix, w_o, h, i, g_final=None):
        return post_mixer_call(x_mix, w_o, h, row(W['norm_ffn'][i]), W['ffn_w_gate'][i], W['ffn_w_up'][i],
                               W['ffn_w_down'][i], p[i].reshape(nt, -1), row(W['norm_ple'][i]),
                               W['ple_w_gate'][i], W['ple_w_proj'][i], tm, tf, g_final=g_final)

    h = post_mixer(o, W['rwkv_w_o'], h, 0)

    slopes = jnp.exp2(-8.0 * jnp.arange(1, N_HEADS + 1, dtype=F32) / N_HEADS)
    if cache is None:
        kt, vt, q, kb, vtb, kmean = kvq_call(
            h, row(W['norm_kv']), row(W['norm_mix'][1]), W['kv_w_k'], W['kv_w_vT'], W['attn_w_q'],
            MOBA_BLOCK, wkt=W['kv_w_kT'], seq=t)
        attn = moba_prompt_call(slopes, q, kb, vtb, kmean.reshape(b, t // MOBA_BLOCK, d), b, t)
        unT = lambda z: jnp.transpose(z.reshape(b, N_HEADS, HEAD_DIM, t), (0, 3, 1, 2))
        k_new, v_new = unT(kt), unT(vt)
    else:
        cache_k, cache_v, page_table = cache
        k_new, v_new, q = kvq_call(h, row(W['norm_kv']), row(W['norm_mix'][1]),
                                   W['kv_w_k'], W['kv_w_v'], W['attn_w_q'], tm)

        def per_head(z):
            z4 = jnp.swapaxes(z.reshape(b, t, N_HEADS, HEAD_DIM), 1, 2)
            return jnp.pad(z4, ((0, 0), (0, 0), (0, 8 - t), (0, 0)))

        q4 = per_head(q)
        pages_t = lambda c: jnp.transpose(c, (0, 2, 3, 1))
        scores, gate = sample_scores_call(page_table, q4, pages_t(cache_k))
        attn4 = sample_attend_call(page_table, slopes, scores, gate, q4, per_head(k_new), per_head(v_new),
                                   pages_t(cache_v))
        attn = jnp.swapaxes(attn4[:, :, :t], 1, 2).reshape(nt, d)
    y = post_mixer(attn, W['attn_w_o'], h, 1, g_final=row(W['norm_final']))
    shp = (b, t, N_HEADS, HEAD_DIM)
    return (y.reshape(b, t, d), wkv_out[None], shift_out[None], k_new.reshape(shp), v_new.reshape(shp))


def kernel(x_prompt, x_sample, p_prompt, p_sample, state_wkv, state_shift, cache_k, cache_v, page_table,
           norm_mix, norm_ffn, norm_ple, norm_kv, norm_final, rwkv_mu, rwkv_w_rkv, rwkv_w_o, rwkv_w0,
           rwkv_w1, rwkv_w2, rwkv_a0, rwkv_a1, rwkv_a2, rwkv_g1, rwkv_g2, rwkv_k_k, rwkv_k_a, rwkv_r_k,
           rwkv_lnx_w, rwkv_lnx_b, attn_w_q, attn_w_o, kv_w_k, kv_w_v, ffn_w_gate, ffn_w_up, ffn_w_down,
           ple_w_proj, ple_w_gate):
    assert norm_mix.shape[0] == 2 and state_wkv.shape[0] == 1, "one RWKV layer then one MoBA layer"
    bf = lambda w: w.astype(BF16)
    W = dict(norm_mix=norm_mix, norm_ffn=norm_ffn, norm_ple=norm_ple, norm_kv=norm_kv, norm_final=norm_final,
             rwkv_mu=rwkv_mu, w_r=bf(rwkv_w_rkv[0, 0]), w_k=bf(rwkv_w_rkv[0, 1]), w_v=bf(rwkv_w_rkv[0, 2]),
             rwkv_w_o=bf(rwkv_w_o[0]), rwkv_w0=rwkv_w0, rwkv_w1=bf(rwkv_w1[0]), rwkv_w2=bf(rwkv_w2[0]),
             rwkv_a0=rwkv_a0, rwkv_a1=bf(rwkv_a1[0]), rwkv_a2=bf(rwkv_a2[0]), rwkv_g1=bf(rwkv_g1[0]),
             rwkv_g2=bf(rwkv_g2[0]), rwkv_k_k=rwkv_k_k, rwkv_k_a=rwkv_k_a, rwkv_r_k=rwkv_r_k,
             rwkv_lnx_w=rwkv_lnx_w, rwkv_lnx_b=rwkv_lnx_b, attn_w_q=bf(attn_w_q[0]), attn_w_o=bf(attn_w_o[0]),
             kv_w_k=bf(kv_w_k), kv_w_v=bf(kv_w_v), kv_w_kT=bf(kv_w_k.T), kv_w_vT=bf(kv_w_v.T),
             ffn_w_gate=bf(ffn_w_gate), ffn_w_up=bf(ffn_w_up), ffn_w_down=bf(ffn_w_down),
             ple_w_proj=bf(ple_w_proj), ple_w_gate=bf(ple_w_gate))
    bp = x_prompt.shape[0]
    d = x_prompt.shape[2]
    wkv0 = jnp.zeros((bp, N_HEADS, HEAD_DIM, HEAD_DIM), F32)
    shift0 = jnp.zeros((bp, d), F32)
    y_p, wkv_p, shift_p, k_p, v_p = _trunk(x_prompt, p_prompt, wkv0, shift0, W)
    y_s, wkv_s, shift_s, k_s, v_s = _trunk(x_sample, p_sample, state_wkv[0], state_shift[0], W,
                                           cache=(cache_k, cache_v, page_table))
    return (y_p, y_s, wkv_p, shift_p, k_p, v_p, wkv_s, shift_s, k_s, v_s)
```
